```python
import jax, jax.numpy as jnp
from jax import lax
import numpy as np

D_MODEL = 1024
BATCH = 8
SEQ = 4096
DEPTH = 2

GRID_W = 64
NA_HEADS = 4
NA_HEAD_DIM = 64
NA_WIN_ROWS = 8
NA_WIN_COLS = 16
MLA_HEADS = 8
MLA_Q_RANK = 384
MLA_KV_RANK = 256
MLA_NOPE_DIM = 64
MLA_ROPE_DIM = 32
MLA_V_DIM = 64
MLA_Q_BLOCK = 128
SWA_HEADS = 4
SWA_KV_HEADS = 2
SWA_HEAD_DIM = 64
SWA_WINDOW = 128
SWA_BLOCK = 128
N_EXPERTS = 16
EXPERT_FF = 512
EC_CAPACITY_FACTOR = 2
PLE_DIM = 256

ROPE_THETA = 10000.0
NORM_EPS = 1e-6
NEG_INF = -1e30

NA_WIDTH = NA_HEADS * NA_HEAD_DIM
MLA_WIDTH = MLA_HEADS * MLA_V_DIM
SWA_WIDTH = SWA_HEADS * SWA_HEAD_DIM
SWA_KV_WIDTH = SWA_KV_HEADS * SWA_HEAD_DIM
MIX_WIDTH = NA_WIDTH + MLA_WIDTH + SWA_WIDTH
IN_SIZES = (NA_WIDTH, NA_WIDTH, NA_WIDTH, MLA_Q_RANK, MLA_KV_RANK, MLA_ROPE_DIM,
            SWA_WIDTH, SWA_KV_WIDTH, SWA_KV_WIDTH)
IN_WIDTH = sum(IN_SIZES)

kernel_name = 'hybrid_na_mla_swa_ec_encoder'


def rms_norm(x, g):
    xf = x.astype(jnp.float32)
    y = xf * lax.rsqrt(jnp.mean(xf * xf, axis=-1, keepdims=True) + NORM_EPS)
    return (y * g.astype(jnp.float32)).astype(x.dtype)


def rope_tables(pos, dim):
    inv = 1.0 / (ROPE_THETA ** (jnp.arange(0, dim, 2, dtype=jnp.float32) / dim))
    ang = pos.astype(jnp.float32)[:, None] * inv[None, :]
    return jnp.cos(ang), jnp.sin(ang)


def apply_rope(x, cos, sin):
    d2 = x.shape[-1] // 2
    x1 = x[..., :d2].astype(jnp.float32)
    x2 = x[..., d2:].astype(jnp.float32)
    c = cos[None, :, None, :]
    s = sin[None, :, None, :]
    return jnp.concatenate([x1 * c - x2 * s, x2 * c + x1 * s], axis=-1).astype(x.dtype)


def neighbourhood_attention(q, k, v, rpb):
    B, S, H, hd = q.shape
    rows = S // GRID_W
    wr = min(NA_WIN_ROWS, rows)
    wc = NA_WIN_COLS
    r = jnp.arange(rows)
    c = jnp.arange(GRID_W)
    r0 = jnp.clip(r - wr // 2, 0, rows - wr)
    c0 = jnp.clip(c - wc // 2, 0, GRID_W - wc)
    key_rows = r0[:, None] + jnp.arange(wr)[None, :]
    qg = q.reshape(B, rows, GRID_W, H, hd)
    kg = k.reshape(B, rows, GRID_W, H, hd)[:, key_rows]
    vg = v.reshape(B, rows, GRID_W, H, hd)[:, key_rows]
    s = jnp.einsum('brchd,briwhd->bhrciw', qg, kg,
                   preferred_element_type=jnp.float32) * (hd ** -0.5)
    kc = jnp.arange(GRID_W)
    col_ok = (kc[None, :] >= c0[:, None]) & (kc[None, :] < c0[:, None] + wc)
    dr = key_rows - r[:, None]
    dc = jnp.clip(kc[None, :] - c[:, None], -(wc - 1), wc - 1)
    bias = rpb[:, dr[:, None, :, None] + NA_WIN_ROWS - 1,
               dc[None, :, None, :] + NA_WIN_COLS - 1]
    s = s + bias.astype(jnp.float32)[None]
    s = jnp.where(col_ok[:, None, :], s, NEG_INF)
    p = jax.nn.softmax(s.reshape(B, H, rows, GRID_W, wr * GRID_W), axis=-1).reshape(s.shape)
    o = jnp.einsum('bhrciw,briwhd->brchd', p.astype(v.dtype), vg)
    return o.reshape(B, S, H * hd)


def latent_attention(c_q, c_kv, k_rope, q_norm, w_q_up, kv_norm, w_kv_up, cos, sin):
    B, S, _ = c_q.shape
    H = MLA_HEADS
    q = (rms_norm(c_q, q_norm) @ w_q_up).reshape(B, S, H, MLA_NOPE_DIM + MLA_ROPE_DIM)
    q_nope = q[..., :MLA_NOPE_DIM]
    q_rope = apply_rope(q[..., MLA_NOPE_DIM:], cos, sin)
    kv = (rms_norm(c_kv, kv_norm) @ w_kv_up).reshape(B, S, H, MLA_NOPE_DIM + MLA_V_DIM)
    k_nope = kv[..., :MLA_NOPE_DIM]
    v = kv[..., MLA_NOPE_DIM:]
    k_r = apply_rope(k_rope[:, :, None, :], cos, sin)[:, :, 0]
    scale = (MLA_NOPE_DIM + MLA_ROPE_DIM) ** -0.5
    nb = S // MLA_Q_BLOCK

    def to_blocks(t):
        return jnp.moveaxis(t.reshape(B, nb, MLA_Q_BLOCK, *t.shape[2:]), 1, 0)

    def block_attn(args):
        qn, qr = args
        s = (jnp.einsum('bqhd,bkhd->bhqk', qn, k_nope, preferred_element_type=jnp.float32)
             + jnp.einsum('bqhr,bkr->bhqk', qr, k_r, preferred_element_type=jnp.float32)) * scale
        p = jax.nn.softmax(s, axis=-1)
        return jnp.einsum('bhqk,bkhd->bqhd', p.astype(v.dtype), v)

    o = lax.map(block_attn, (to_blocks(q_nope), to_blocks(q_rope)))
    return jnp.moveaxis(o, 0, 1).reshape(B, S, H * MLA_V_DIM)


def window_gqa_sink(q, k, v, sink, cos, sin):
    B, S, H, hd = q.shape
    KVH = k.shape[2]
    G = H // KVH
    L = SWA_BLOCK
    nb = S // L
    q = apply_rope(q, cos, sin)
    k = apply_rope(k, cos, sin)
    pad = ((0, 0), (L, L), (0, 0), (0, 0))
    kp = jnp.pad(k, pad).reshape(B, nb + 2, L, KVH, hd)
    vp = jnp.pad(v, pad).reshape(B, nb + 2, L, KVH, hd)
    kb = jnp.concatenate([kp[:, :-2], kp[:, 1:-1], kp[:, 2:]], axis=2)
    vb = jnp.concatenate([vp[:, :-2], vp[:, 1:-1], vp[:, 2:]], axis=2)
    qb = q.reshape(B, nb, L, KVH, G, hd)
    s = jnp.einsum('bnqkgd,bnjkd->bkgnqj', qb, kb,
                   preferred_element_type=jnp.float32) * (hd ** -0.5)
    qi = jnp.arange(L)[:, None]
    kj = jnp.arange(3 * L)[None, :]
    rel = kj - L - qi
    kpos = (jnp.arange(nb)[:, None, None] - 1) * L + kj[None]
    valid = (jnp.abs(rel)[None] <= SWA_WINDOW) & (kpos >= 0) & (kpos < S)
    s = jnp.where(valid, s, NEG_INF)
    sink_l = sink.astype(jnp.float32).reshape(KVH, G)[None, :, :, None, None, None]
    m = jnp.maximum(jnp.max(s, axis=-1, keepdims=True), sink_l)
    e = jnp.exp(s - m)
    p = e / (jnp.sum(e, axis=-1, keepdims=True) + jnp.exp(sink_l - m))
    o = jnp.einsum('bkgnqj,bnjkd->bnqkgd', p.astype(v.dtype), vb)
    return o.reshape(B, S, H * hd)


def expert_choice_ffn(h, w_router, w_gate, w_up, w_down):
    B, S, D = h.shape
    cap = EC_CAPACITY_FACTOR * S // N_EXPERTS
    logits = jnp.einsum('bsd,de->bse', h, w_router, preferred_element_type=jnp.float32)
    aff = jax.nn.softmax(logits, axis=-1)
    g, idx = lax.top_k(jnp.swapaxes(aff, 1, 2), cap)
    xs = jax.vmap(lambda hb, ib: hb[ib])(h, idx)
    a = jnp.einsum('becd,edf->becf', xs, w_gate)
    u = jnp.einsum('becd,edf->becf', xs, w_up)
    y = jnp.einsum('becf,efd->becd', jax.nn.silu(a) * u, w_down)
    y = y * g[..., None].astype(y.dtype)
    return jax.vmap(lambda ib, yb: jnp.zeros((S, D), yb.dtype)
                    .at[ib.reshape(-1)].add(yb.reshape(-1, D)))(idx, y)


def setup_inputs(seed: int = 0) -> dict:
    key = jax.random.key(seed)
    ks = jax.random.split(key, 24)
    f32 = jnp.float32

    def nrm(k, shape, scale):
        return jax.random.normal(k, shape, f32) * scale

    def gain(k, shape):
        return 1.0 + 0.05 * jax.random.normal(k, shape, f32)

    return {
        'x': nrm(ks[0], (BATCH, SEQ, D_MODEL), 1.0),
        'p': nrm(ks[1], (DEPTH, BATCH, SEQ, PLE_DIM), 1.0),
        'attn_norm': gain(ks[2], (DEPTH, D_MODEL)),
        'w_in': nrm(ks[3], (DEPTH, D_MODEL, IN_WIDTH), D_MODEL ** -0.5),
        'na_rpb': nrm(ks[4], (DEPTH, NA_HEADS, 2 * NA_WIN_ROWS - 1, 2 * NA_WIN_COLS - 1), 0.1),
        'mla_q_norm': gain(ks[5], (DEPTH, MLA_Q_RANK)),
        'mla_w_q_up': nrm(ks[6], (DEPTH, MLA_Q_RANK, MLA_HEADS * (MLA_NOPE_DIM + MLA_ROPE_DIM)), MLA_Q_RANK ** -0.5),
        'mla_kv_norm': gain(ks[7], (DEPTH, MLA_KV_RANK)),
        'mla_w_kv_up': nrm(ks[8], (DEPTH, MLA_KV_RANK, MLA_HEADS * (MLA_NOPE_DIM + MLA_V_DIM)), MLA_KV_RANK ** -0.5),
        'swa_sink': nrm(ks[9], (DEPTH, SWA_HEADS), 1.0),
        'w_out': nrm(ks[10], (DEPTH, MIX_WIDTH, D_MODEL), MIX_WIDTH ** -0.5),
        'ffn_norm': gain(ks[11], (DEPTH, D_MODEL)),
        'w_router': nrm(ks[12], (DEPTH, D_MODEL, N_EXPERTS), D_MODEL ** -0.5),
        'w_gate_e': nrm(ks[13], (DEPTH, N_EXPERTS, D_MODEL, EXPERT_FF), D_MODEL ** -0.5),
        'w_up_e': nrm(ks[14], (DEPTH, N_EXPERTS, D_MODEL, EXPERT_FF), D_MODEL ** -0.5),
        'w_down_e': nrm(ks[15], (DEPTH, N_EXPERTS, EXPERT_FF, D_MODEL), EXPERT_FF ** -0.5),
        'ple_norm': gain(ks[16], (DEPTH, D_MODEL)),
        'w_ple_gate': nrm(ks[17], (DEPTH, D_MODEL, D_MODEL), D_MODEL ** -0.5),
        'w_ple_proj': nrm(ks[18], (DEPTH, PLE_DIM, D_MODEL), PLE_DIM ** -0.5),
        'final_norm': gain(ks[19], (D_MODEL,)),
    }


def reference(x, p, attn_norm, w_in, na_rpb, mla_q_norm, mla_w_q_up, mla_kv_norm, mla_w_kv_up,
              swa_sink, w_out, ffn_norm, w_router, w_gate_e, w_up_e, w_down_e,
              ple_norm, w_ple_gate, w_ple_proj, final_norm):
    B, S, D = x.shape
    pos = jnp.arange(S)
    cos_mla, sin_mla = rope_tables(pos, MLA_ROPE_DIM)
    cos_swa, sin_swa = rope_tables(pos, SWA_HEAD_DIM)
    offsets = [int(o) for o in np.cumsum(IN_SIZES)[:-1]]
    h = x
    for i in range(DEPTH):
        hn = rms_norm(h, attn_norm[i])
        proj = hn @ w_in[i]
        na_q, na_k, na_v, mla_cq, mla_ckv, mla_kr, swa_q, swa_k, swa_v = jnp.split(proj, offsets, axis=-1)
        o_na = neighbourhood_attention(
            na_q.reshape(B, S, NA_HEADS, NA_HEAD_DIM),
            na_k.reshape(B, S, NA_HEADS, NA_HEAD_DIM),
            na_v.reshape(B, S, NA_HEADS, NA_HEAD_DIM), na_rpb[i])
        o_mla = latent_attention(mla_cq, mla_ckv, mla_kr, mla_q_norm[i], mla_w_q_up[i],
                                 mla_kv_norm[i], mla_w_kv_up[i], cos_mla, sin_mla)
        o_swa = window_gqa_sink(
            swa_q.reshape(B, S, SWA_HEADS, SWA_HEAD_DIM),
            swa_k.reshape(B, S, SWA_KV_HEADS, SWA_HEAD_DIM),
            swa_v.reshape(B, S, SWA_KV_HEADS, SWA_HEAD_DIM), swa_sink[i], cos_swa, sin_swa)
        h = h + jnp.concatenate([o_na, o_mla, o_swa], axis=-1) @ w_out[i]
        h = h + expert_choice_ffn(rms_norm(h, ffn_norm[i]), w_router[i],
                                  w_gate_e[i], w_up_e[i], w_down_e[i])
        gate = jax.nn.sigmoid(rms_norm(h, ple_norm[i]) @ w_ple_gate[i])
        h = h + gate * (p[i] @ w_ple_proj[i])
    return rms_norm(h, final_norm)
```

```python
import functools
import math

import jax
import jax.numpy as jnp
from jax import lax
from jax.experimental import pallas as pl
from jax.experimental.pallas import tpu as pltpu

F32 = jnp.float32
BF16 = jnp.bfloat16

GRID_W = 64
NA_HEADS = 4
NA_HEAD_DIM = 64
NA_WIN_ROWS = 8
NA_WIN_COLS = 16
MLA_HEADS = 8
MLA_Q_RANK = 384
MLA_KV_RANK = 256
MLA_NOPE_DIM = 64
MLA_ROPE_DIM = 32
MLA_V_DIM = 64
SWA_HEADS = 4
SWA_KV_HEADS = 2
SWA_HEAD_DIM = 64
SWA_WINDOW = 128
SWA_BLOCK = 128
N_EXPERTS = 16
EXPERT_FF = 512
EC_CAPACITY_FACTOR = 2
PLE_DIM = 256
ROPE_THETA = 10000.0
NORM_EPS = 1e-6
NEG_INF = -1e30

NA_WIDTH = NA_HEADS * NA_HEAD_DIM
MLA_WIDTH = MLA_HEADS * MLA_V_DIM
SWA_WIDTH = SWA_HEADS * SWA_HEAD_DIM
SWA_KV_WIDTH = SWA_KV_HEADS * SWA_HEAD_DIM
IN_SIZES = (NA_WIDTH, NA_WIDTH, NA_WIDTH, MLA_Q_RANK, MLA_KV_RANK, MLA_ROPE_DIM,
            SWA_WIDTH, SWA_KV_WIDTH, SWA_KV_WIDTH)

LANE = 128
MLA_HEAD_PAD = LANE
VMEM_LIMIT = 56 * 1024 * 1024

TOK_TILE = 512
NA_QROWS = 4
NA_KROWS = 12
MLA_TQ = 256


def _cparams(sem):
    return pltpu.CompilerParams(dimension_semantics=sem, vmem_limit_bytes=VMEM_LIMIT)


def _dot(a, b):
    return jnp.dot(a, b, preferred_element_type=F32)


def _dot_t(a, b):
    return lax.dot_general(a, b, (((1,), (1,)), ((), ())), preferred_element_type=F32)


def _rms(x, g):
    return x * lax.rsqrt(jnp.mean(x * x, axis=-1, keepdims=True) + NORM_EPS) * g


def _inproj_kernel(h_ref, g_ref, gq_ref, gkv_ref, wna_ref, wmc_ref, wkr_ref, wswa_ref,
                   wq_ref, wkv_ref, cq_ref, sq_ref, ck_ref, sk_ref, cs_ref, ss_ref,
                   naq_ref, nak_ref, nav_ref, mq_ref, mk_ref, mv_ref,
                   swq_ref, swk_ref, swv_ref):
    xn = _rms(h_ref[...], g_ref[...]).astype(BF16)

    pna = _dot(xn, wna_ref[...])
    naq_ref[...] = pna[:, :NA_WIDTH].astype(BF16)
    nak_ref[...] = pna[:, NA_WIDTH:2 * NA_WIDTH].astype(BF16)
    nav_ref[...] = pna[:, 2 * NA_WIDTH:].astype(BF16)

    pmc = _dot(xn, wmc_ref[...])
    cqn = _rms(pmc[:, :MLA_Q_RANK], gq_ref[...]).astype(BF16)
    ckvn = _rms(pmc[:, MLA_Q_RANK:], gkv_ref[...]).astype(BF16)
    q2 = _dot(cqn, wq_ref[...])
    kv2 = _dot(ckvn, wkv_ref[...])
    pkr = _dot(xn, wkr_ref[...])
    kr = pkr[:, :LANE] * ck_ref[...] + pkr[:, LANE:] * sk_ref[...]
    hw = MLA_HEADS * MLA_HEAD_PAD
    cq = cq_ref[...]
    sq = sq_ref[...]
    for hd in range(MLA_HEADS):
        lo, hi = hd * MLA_HEAD_PAD, (hd + 1) * MLA_HEAD_PAD
        mq_ref[:, lo:hi] = (q2[:, lo:hi] * cq + q2[:, hw + lo:hw + hi] * sq).astype(BF16)
        mk_ref[:, lo:hi] = (kv2[:, lo:hi] + kr).astype(BF16)
    mv_ref[...] = kv2[:, hw:].astype(BF16)

    ps = _dot(xn, wswa_ref[...])
    cs = cs_ref[...]
    ss = ss_ref[...]
    for j in range(SWA_WIDTH // LANE):
        lo, hi = j * LANE, (j + 1) * LANE
        swq_ref[:, lo:hi] = (ps[:, lo:hi] * cs
                             + ps[:, SWA_WIDTH + lo:SWA_WIDTH + hi] * ss).astype(BF16)
    o = 2 * SWA_WIDTH
    swk_ref[...] = (ps[:, o:o + SWA_KV_WIDTH] * cs
                    + ps[:, o + SWA_KV_WIDTH:o + 2 * SWA_KV_WIDTH] * ss).astype(BF16)
    swv_ref[...] = ps[:, o + 2 * SWA_KV_WIDTH:].astype(BF16)


def _inproj(h, lw, tabs, seq):
    t, d = h.shape
    tm = TOK_TILE
    nt = t // tm
    spt = seq // tm

    def full(a):
        return pl.BlockSpec(a.shape, lambda i: (0,) * a.ndim)

    def tab(a):
        return pl.BlockSpec((tm, a.shape[1]), lambda i: (i % spt, 0))

    def tok(w):
        return pl.BlockSpec((tm, w), lambda i: (i, 0))

    weights = [lw['g_attn'], lw['g_q'], lw['g_kv'], lw['w_na'], lw['w_mc'], lw['w_kr'],
               lw['w_swa'], lw['w_q_up'], lw['w_kv_up']]
    tables = [tabs['cq'], tabs['sq'], tabs['ck'], tabs['sk'], tabs['cs'], tabs['ss']]
    widths = [NA_WIDTH, NA_WIDTH, NA_WIDTH, MLA_HEADS * MLA_HEAD_PAD, MLA_HEADS * MLA_HEAD_PAD,
              MLA_HEADS * MLA_HEAD_PAD, SWA_WIDTH, SWA_KV_WIDTH, SWA_KV_WIDTH]
    return pl.pallas_call(
        _inproj_kernel,
        grid=(nt,),
        in_specs=[tok(d)] + [full(w) for w in weights] + [tab(a) for a in tables],
        out_specs=[tok(w) for w in widths],
        out_shape=[jax.ShapeDtypeStruct((t, w), BF16) for w in widths],
        compiler_params=_cparams(("parallel",)),
        name="inproj",
    )(h, *weights, *tables)


def _na_kernel(q_ref, k_ref, v_ref, tbl_ref, o_ref, *, n_blocks):
    blk = pl.program_id(1)
    last_start = (n_blocks * NA_QROWS - NA_KROWS)
    ks = jnp.clip(blk * NA_QROWS - NA_WIN_ROWS // 2, 0, last_start) * GRID_W
    ks = pl.multiple_of(ks, GRID_W)
    nk = NA_KROWS * GRID_W
    outs = []
    for hd in range(NA_HEADS):
        lo, hi = hd * NA_HEAD_DIM, (hd + 1) * NA_HEAD_DIM
        q = q_ref[0, :, lo:hi]
        k = k_ref[0, pl.ds(ks, nk), lo:hi]
        v = v_ref[0, pl.ds(ks, nk), lo:hi]
        s = _dot_t(q, k) + tbl_ref[0, hd]
        m = jnp.max(s, axis=-1, keepdims=True)
        e = jnp.exp(s - m)
        l = jnp.sum(e, axis=-1, keepdims=True)
        outs.append(_dot(e.astype(BF16), v) / l)
    o_ref[0] = jnp.concatenate(outs, axis=-1).astype(BF16)


def _na_attention(q, k, v, tbl):
    b, s, w = q.shape
    rows = s // GRID_W
    nb = rows // NA_QROWS
    tq = NA_QROWS * GRID_W

    def tbl_map(bi, blk):
        sel = jnp.where(blk == 0, 0, jnp.where(blk == nb - 1, 2, 1))
        return (sel, 0, 0, 0)

    return pl.pallas_call(
        functools.partial(_na_kernel, n_blocks=nb),
        grid=(b, nb),
        in_specs=[pl.BlockSpec((1, tq, w), lambda bi, blk: (bi, blk, 0)),
                  pl.BlockSpec((1, s, w), lambda bi, blk: (bi, 0, 0)),
                  pl.BlockSpec((1, s, w), lambda bi, blk: (bi, 0, 0)),
                  pl.BlockSpec((1,) + tbl.shape[1:], tbl_map)],
        out_specs=pl.BlockSpec((1, tq, w), lambda bi, blk: (bi, blk, 0)),
        out_shape=jax.ShapeDtypeStruct((b, s, w), BF16),
        compiler_params=_cparams(("parallel", "arbitrary")),
        name="na_attn",
    )(q, k, v, tbl)


def _mla_kernel(q_ref, k_ref, v_ref, o_ref):
    acc = None
    for hh in range(2):
        lo, hi = hh * MLA_HEAD_PAD, (hh + 1) * MLA_HEAD_PAD
        q = q_ref[0, :, lo:hi]
        k = k_ref[0, :, lo:hi]
        v = v_ref[0, :, lo:hi]
        s = _dot_t(q, k)
        m = jnp.max(s, axis=-1, keepdims=True)
        e = jnp.exp2(s - m)
        l = jnp.sum(e, axis=-1, keepdims=True)
        o = _dot(e.astype(BF16), v) / l
        acc = o if acc is None else acc + o
    o_ref[0] = acc.astype(BF16)


def _mla_attention(q, k, v):
    b, s, _ = q.shape
    pw = 2 * MLA_HEAD_PAD
    npair = MLA_HEADS // 2
    nq = s // MLA_TQ
    return pl.pallas_call(
        _mla_kernel,
        grid=(b, npair, nq),
        in_specs=[pl.BlockSpec((1, MLA_TQ, pw), lambda bi, hp, qi: (bi, qi, hp)),
                  pl.BlockSpec((1, s, pw), lambda bi, hp, qi: (bi, 0, hp)),
                  pl.BlockSpec((1, s, pw), lambda bi, hp, qi: (bi, 0, hp))],
        out_specs=pl.BlockSpec((1, MLA_TQ, LANE), lambda bi, hp, qi: (bi, qi, hp)),
        out_shape=jax.ShapeDtypeStruct((b, s, MLA_WIDTH), BF16),
        compiler_params=_cparams(("parallel", "parallel", "arbitrary")),
        name="mla_attn",
    )(q, k, v)


def _swa_kernel(sink_ref, q_ref, k_ref, v_ref, o_ref, *, seq):
    n = pl.program_id(1)
    blk = SWA_BLOCK
    nk = 3 * blk
    ks = pl.multiple_of(jnp.clip((n - 1) * blk, 0, seq - nk), blk)
    group = SWA_HEADS // SWA_KV_HEADS
    rows = group * blk
    ri = lax.broadcasted_iota(jnp.int32, (rows, nk), 0)
    ci = lax.broadcasted_iota(jnp.int32, (rows, nk), 1)
    rel = (ks + ci) - (n * blk + (ri & (blk - 1)))
    valid = jnp.abs(rel) <= SWA_WINDOW
    rcol = lax.broadcasted_iota(jnp.int32, (rows, 1), 0)
    outs = []
    for kvh in range(SWA_KV_HEADS):
        qs = []
        sink = jnp.zeros((rows, 1), F32)
        for g in range(group):
            hd = kvh * group + g
            qs.append(q_ref[0, :, hd * SWA_HEAD_DIM:(hd + 1) * SWA_HEAD_DIM])
            sink = jnp.where((rcol >= g * blk) & (rcol < (g + 1) * blk), sink_ref[hd], sink)
        q = jnp.concatenate(qs, axis=0)
        lo, hi = kvh * SWA_HEAD_DIM, (kvh + 1) * SWA_HEAD_DIM
        k = k_ref[0, pl.ds(ks, nk), lo:hi]
        v = v_ref[0, pl.ds(ks, nk), lo:hi]
        s = jnp.where(valid, _dot_t(q, k), NEG_INF)
        m = jnp.maximum(jnp.max(s, axis=-1, keepdims=True), sink)
        e = jnp.exp(s - m)
        den = jnp.sum(e, axis=-1, keepdims=True) + jnp.exp(sink - m)
        o = _dot(e.astype(BF16), v) / den
        for g in range(group):
            outs.append(o[g * blk:(g + 1) * blk])
    o_ref[0] = jnp.concatenate(outs, axis=-1).astype(BF16)


def _swa_attention(q, k, v, sink):
    b, s, w = q.shape
    kw = k.shape[2]
    nb = s // SWA_BLOCK
    return pl.pallas_call(
        functools.partial(_swa_kernel, seq=s),
        grid=(b, nb),
        in_specs=[pl.BlockSpec(memory_space=pltpu.SMEM),
                  pl.BlockSpec((1, SWA_BLOCK, w), lambda bi, n: (bi, n, 0)),
                  pl.BlockSpec((1, s, kw), lambda bi, n: (bi, 0, 0)),
                  pl.BlockSpec((1, s, kw), lambda bi, n: (bi, 0, 0))],
        out_specs=pl.BlockSpec((1, SWA_BLOCK, w), lambda bi, n: (bi, n, 0)),
        out_shape=jax.ShapeDtypeStruct((b, s, w), BF16),
        compiler_params=_cparams(("parallel", "arbitrary")),
        name="swa_attn",
    )(sink, q, k, v)


def _outproj_kernel(h_ref, ona_ref, omla_ref, oswa_ref, wna_ref, wmla_ref, wswa_ref,
                    g_ref, wr_ref, h1_ref, aff_ref):
    h1 = (h_ref[...] + _dot(ona_ref[...], wna_ref[...]) + _dot(omla_ref[...], wmla_ref[...])
          + _dot(oswa_ref[...], wswa_ref[...]))
    h1_ref[...] = h1
    hn = _rms(h1, g_ref[...])
    hn_hi = hn.astype(BF16)
    hn_lo = (hn - hn_hi.astype(F32)).astype(BF16)
    wr = wr_ref[...]
    p_hi = _dot_t(wr, hn_hi)
    p_lo = _dot_t(wr, hn_lo)
    logits = p_hi[:N_EXPERTS] + p_hi[N_EXPERTS:] + p_lo[:N_EXPERTS]
    m = jnp.max(logits, axis=0, keepdims=True)
    e = jnp.exp(logits - m)
    aff_ref[0] = e / jnp.sum(e, axis=0, keepdims=True)


def _outproj(h, o_na, o_mla, o_swa, lw, batch, seq):
    t, d = h.shape
    tm = TOK_TILE
    spt = seq // tm

    def full(a):
        return pl.BlockSpec(a.shape, lambda i: (0,) * a.ndim)

    def tok(w):
        return pl.BlockSpec((tm, w), lambda i: (i, 0))

    weights = [lw['w_out_na'], lw['w_out_mla'], lw['w_out_swa'], lw['g_ffn'], lw['w_router']]
    return pl.pallas_call(
        _outproj_kernel,
        grid=(t // tm,),
        in_specs=[tok(d), tok(NA_WIDTH), tok(MLA_WIDTH), tok(SWA_WIDTH)] + [full(w) for w in weights],
        out_specs=[tok(d), pl.BlockSpec((1, N_EXPERTS, tm), lambda i: (i // spt, 0, i % spt))],
        out_shape=[jax.ShapeDtypeStruct((t, d), F32),
                   jax.ShapeDtypeStruct((batch, N_EXPERTS, seq), F32)],
        compiler_params=_cparams(("parallel",)),
        name="outproj_router",
    )(h, o_na, o_mla, o_swa, *weights)


def _select_kernel(aff_ref, tri_ref, idx_ref, gate_ref, pos_scr, aff_scr, *, cap):
    x = aff_ref[0]
    ne, seq = x.shape
    capf = float(cap)

    def count_ge(v):
        return jnp.sum(jnp.where(x >= v, 1.0, 0.0), axis=1, keepdims=True)

    def search(i, tb):
        cand = tb | jnp.left_shift(jnp.int32(1), 30 - i)
        return jnp.where(count_ge(pltpu.bitcast(cand, F32)) >= capf, cand, tb)

    tb = lax.fori_loop(0, 31, search, jnp.zeros((ne, 1), jnp.int32))

    def refine(i, lohi):
        lo, hi = lohi
        mid = 0.5 * (lo + hi)
        ok = count_ge(mid) >= capf
        return jnp.where(ok, mid, lo), jnp.where(ok, hi, mid)

    thr, _ = lax.fori_loop(0, 24, refine, (pltpu.bitcast(tb, F32), pltpu.bitcast(tb + 1, F32)))
    gt = jnp.where(x > thr, 1.0, 0.0)
    eq = jnp.where(x == thr, 1.0, 0.0)
    need = capf - jnp.sum(gt, axis=1, keepdims=True)
    tri = tri_ref[...]

    def cumsum(mask):
        outs = []
        off = jnp.zeros((ne, 1), F32)
        for c in range(seq // LANE):
            cs = _dot(mask[:, c * LANE:(c + 1) * LANE].astype(BF16), tri) + off
            outs.append(cs)
            off = cs[:, LANE - 1:LANE]
        return jnp.concatenate(outs, axis=1)

    tie_rank = cumsum(eq) - eq
    sel = gt + eq * jnp.where(tie_rank < need, 1.0, 0.0)
    pos = jnp.where(sel > 0.0, cumsum(sel) - 1.0, -1.0)
    for e in range(ne):
        pos_scr[e] = pos[e:e + 1, :]
        aff_scr[e] = x[e:e + 1, :]

    n_chunks = seq // LANE
    slot_blocks = cap // LANE

    def per_expert(e, carry):
        for sb in range(slot_blocks):
            slot = (lax.broadcasted_iota(jnp.int32, (LANE, LANE), 0) + sb * LANE).astype(F32)

            def chunk(c, acc):
                a_idx, a_g = acc
                off = pl.multiple_of(c * LANE, LANE)
                hit = pos_scr[e, :, pl.ds(off, LANE)] == slot
                tok = (lax.broadcasted_iota(jnp.int32, (LANE, LANE), 1) + off).astype(F32)
                a_idx = a_idx + jnp.where(hit, tok, 0.0)
                a_g = a_g + jnp.where(hit, aff_scr[e, :, pl.ds(off, LANE)], 0.0)
                return a_idx, a_g

            zero = jnp.zeros((LANE, LANE), F32)
            a_idx, a_g = lax.fori_loop(0, n_chunks, chunk, (zero, zero))
            idx_ref[0, e, sb * LANE:(sb + 1) * LANE, :] = (
                jnp.sum(a_idx, axis=1, keepdims=True).astype(jnp.int32))
            gate_ref[0, e, sb * LANE:(sb + 1) * LANE, :] = jnp.sum(a_g, axis=1, keepdims=True)
        return carry

    lax.fori_loop(0, ne, per_expert, 0)


def _select(aff, tri, cap):
    b, ne, seq = aff.shape
    return pl.pallas_call(
        functools.partial(_select_kernel, cap=cap),
        grid=(b,),
        in_specs=[pl.BlockSpec((1, ne, seq), lambda bi: (bi, 0, 0)),
                  pl.BlockSpec(tri.shape, lambda bi: (0, 0))],
        out_specs=[pl.BlockSpec((1, ne, cap, 1), lambda bi: (bi, 0, 0, 0)),
                   pl.BlockSpec((1, ne, cap, 1), lambda bi: (bi, 0, 0, 0))],
        out_shape=[jax.ShapeDtypeStruct((b, ne, cap, 1), jnp.int32),
                   jax.ShapeDtypeStruct((b, ne, cap, 1), F32)],
        scratch_shapes=[pltpu.VMEM((ne, 1, seq), F32), pltpu.VMEM((ne, 1, seq), F32)],
        compiler_params=_cparams(("parallel",)),
        name="ec_select",
    )(aff, tri)


def _ffn_kernel(idx_ref, h_ref, gate_ref, g_ref, wg_ref, wu_ref, wd_ref, y_ref, xs_scr, *, cap):
    def gather(i, carry):
        t = idx_ref[0, 0, i]
        xs_scr[pl.ds(i, 1), :] = h_ref[0, pl.ds(t, 1), :]
        return carry

    lax.fori_loop(0, cap, gather, 0, unroll=8)
    xn = _rms(xs_scr[...], g_ref[...]).astype(BF16)
    a = _dot(xn, wg_ref[0])
    u = _dot(xn, wu_ref[0])
    mid = (a * jax.nn.sigmoid(a) * u).astype(BF16)
    y_ref[0, 0] = _dot(mid, wd_ref[0]) * gate_ref[0, 0]


def _ffn(idx, h1, gate, lw, cap):
    b, seq, d = h1.shape
    ne = N_EXPERTS
    ff = EXPERT_FF
    idx3 = idx.reshape(b * ne, 1, cap)
    return pl.pallas_call(
        functools.partial(_ffn_kernel, cap=cap),
        grid=(b, ne),
        in_specs=[pl.BlockSpec((1, 1, cap), lambda bi, e: (bi * ne + e, 0, 0),
                               memory_space=pltpu.SMEM),
                  pl.BlockSpec((1, seq, d), lambda bi, e: (bi, 0, 0)),
                  pl.BlockSpec((1, 1, cap, 1), lambda bi, e: (bi, e, 0, 0)),
                  pl.BlockSpec((1, d), lambda bi, e: (0, 0)),
                  pl.BlockSpec((1, d, ff), lambda bi, e: (e, 0, 0)),
                  pl.BlockSpec((1, d, ff), lambda bi, e: (e, 0, 0)),
                  pl.BlockSpec((1, ff, d), lambda bi, e: (e, 0, 0))],
        out_specs=pl.BlockSpec((1, 1, cap, d), lambda bi, e: (bi, e, 0, 0)),
        out_shape=jax.ShapeDtypeStruct((b, ne, cap, d), F32),
        scratch_shapes=[pltpu.VMEM((cap, d), F32)],
        compiler_params=_cparams(("parallel", "arbitrary")),
        name="ec_ffn",
    )(idx3, h1, gate, lw['g_ffn'], lw['w_gate'], lw['w_up'], lw['w_down'])


SCATTER_ROWS = 8


def _scatter_kernel(idx_ref, h_ref, y_ref, o_ref, *, cap):
    @pl.when(pl.program_id(2) == 0)
    def _():
        o_ref[...] = h_ref[...]

    def group(j, carry):
        base = pl.multiple_of(j * SCATTER_ROWS, SCATTER_ROWS)
        toks = [idx_ref[0, 0, base + r] for r in range(SCATTER_ROWS)]
        rows = [o_ref[0, pl.ds(toks[r], 1), :] for r in range(SCATTER_ROWS)]
        for r in range(SCATTER_ROWS):
            o_ref[0, pl.ds(toks[r], 1), :] = rows[r] + y_ref[0, 0, pl.ds(base + r, 1), :]
        return carry

    lax.fori_loop(0, cap // SCATTER_ROWS, group, 0)


def _scatter(idx, h1, y, cap):
    b, seq, d = h1.shape
    ne = N_EXPERTS
    dh = d // 2
    idx3 = idx.reshape(b * ne, 1, cap)
    return pl.pallas_call(
        functools.partial(_scatter_kernel, cap=cap),
        grid=(b, 2, ne),
        in_specs=[pl.BlockSpec((1, 1, cap), lambda bi, j, e: (bi * ne + e, 0, 0),
                               memory_space=pltpu.SMEM),
                  pl.BlockSpec((1, seq, dh), lambda bi, j, e: (bi, 0, j)),
                  pl.BlockSpec((1, 1, cap, dh), lambda bi, j, e: (bi, e, 0, j))],
        out_specs=pl.BlockSpec((1, seq, dh), lambda bi, j, e: (bi, 0, j)),
        out_shape=jax.ShapeDtypeStruct((b, seq, d), F32),
        compiler_params=_cparams(("parallel", "parallel", "arbitrary")),
        name="ec_scatter",
    )(idx3, h1, y)


def _ple_kernel(h_ref, p_ref, g_ref, wg_ref, wp_ref, gf_ref, o_ref, *, final):
    h = h_ref[...]
    gate = jax.nn.sigmoid(_dot(_rms(h, g_ref[...]).astype(BF16), wg_ref[...]))
    out = h + gate * _dot(p_ref[...].astype(BF16), wp_ref[...])
    if final:
        out = _rms(out, gf_ref[...])
    o_ref[...] = out


def _ple(h, p, lw, g_final, final):
    t, d = h.shape
    tm = TOK_TILE

    def full(a):
        return pl.BlockSpec(a.shape, lambda i: (0,) * a.ndim)

    def tok(w):
        return pl.BlockSpec((tm, w), lambda i: (i, 0))

    weights = [lw['g_ple'], lw['w_ple_gate'], lw['w_ple_proj'], g_final]
    return pl.pallas_call(
        functools.partial(_ple_kernel, final=final),
        grid=(t // tm,),
        in_specs=[tok(d), tok(p.shape[1])] + [full(w) for w in weights],
        out_specs=tok(d),
        out_shape=jax.ShapeDtypeStruct((t, d), F32),
        compiler_params=_cparams(("parallel",)),
        name="ple",
    )(h, p, *weights)


def _rot_cols(w, d):
    g = w.reshape(w.shape[:-1] + (w.shape[-1] // d, d))
    return jnp.concatenate([-g[..., d // 2:], g[..., :d // 2]], axis=-1).reshape(w.shape)


def _rope_tables(seq):
    pos = jnp.arange(seq, dtype=F32)

    def cs(dim):
        inv = 1.0 / (ROPE_THETA ** (jnp.arange(0, dim, 2, dtype=F32) / dim))
        ang = pos[:, None] * inv[None, :]
        return jnp.cos(ang), jnp.sin(ang)

    cm, sm = cs(MLA_ROPE_DIM)
    cw, sw = cs(SWA_HEAD_DIM)
    z = lambda n: jnp.zeros((seq, n), F32)
    one = jnp.ones((seq, MLA_NOPE_DIM), F32)
    pad = MLA_HEAD_PAD - MLA_NOPE_DIM - MLA_ROPE_DIM
    qs = (MLA_NOPE_DIM + MLA_ROPE_DIM) ** -0.5 * math.log2(math.e)
    return {
        'cq': qs * jnp.concatenate([one, cm, cm, z(pad)], axis=1),
        'sq': qs * jnp.concatenate([z(MLA_NOPE_DIM), sm, sm, z(pad)], axis=1),
        'ck': jnp.concatenate([z(MLA_NOPE_DIM), cm, cm, z(pad)], axis=1),
        'sk': jnp.concatenate([z(MLA_NOPE_DIM), sm, sm, z(pad)], axis=1),
        'cs': jnp.concatenate([cw, cw] * (LANE // SWA_HEAD_DIM), axis=1),
        'ss': jnp.concatenate([sw, sw] * (LANE // SWA_HEAD_DIM), axis=1),
    }


def _na_bias_tables(rpb):
    nh = rpb.shape[0]
    qi = jnp.arange(NA_QROWS)[:, None]
    ki = jnp.arange(NA_KROWS)[None, :]
    half = NA_WIN_ROWS // 2
    cases = [
        ((ki < NA_WIN_ROWS) & (qi >= 0), ki - qi),
        ((ki >= qi) & (ki < qi + NA_WIN_ROWS), ki - half - qi),
        ((ki >= NA_KROWS - NA_WIN_ROWS) & (qi >= 0), ki - (NA_KROWS - NA_QROWS) - qi),
    ]
    row_ok = jnp.stack([ok for ok, _ in cases])
    dri = jnp.stack([jnp.clip(dr + NA_WIN_ROWS - 1, 0, 2 * NA_WIN_ROWS - 2) for _, dr in cases])
    c = jnp.arange(GRID_W)[:, None]
    kc = jnp.arange(GRID_W)[None, :]
    c0 = jnp.clip(c - NA_WIN_COLS // 2, 0, GRID_W - NA_WIN_COLS)
    col_ok = (kc >= c0) & (kc < c0 + NA_WIN_COLS)
    dci = jnp.clip(kc - c, -(NA_WIN_COLS - 1), NA_WIN_COLS - 1) + NA_WIN_COLS - 1
    one_r = jax.nn.one_hot(dri, 2 * NA_WIN_ROWS - 1, dtype=F32)
    one_c = jax.nn.one_hot(dci, 2 * NA_WIN_COLS - 1, dtype=F32)
    bias = jnp.einsum('xqka,hab,cnb->xhqckn', one_r, rpb.astype(F32), one_c,
                      precision=lax.Precision.HIGHEST)
    ok = row_ok[:, None, :, None, :, None] & col_ok[None, None, None, :, None, :]
    tab = jnp.where(ok, bias, NEG_INF)
    return tab.reshape(3, nh, NA_QROWS * GRID_W, NA_KROWS * GRID_W)


def _layer_weights(i, attn_norm, w_in, mla_q_norm, mla_w_q_up, mla_kv_norm, mla_w_kv_up, w_out,
                   ffn_norm, w_router, w_gate_e, w_up_e, w_down_e, ple_norm, w_ple_gate,
                   w_ple_proj):
    offs = [0]
    for n in IN_SIZES:
        offs.append(offs[-1] + n)
    parts = [w_in[i][:, offs[j]:offs[j + 1]] for j in range(len(IN_SIZES))]
    na_q, na_k, na_v, m_cq, m_ckv, m_kr, s_q, s_k, s_v = parts
    d = w_in.shape[1]
    na_scale = NA_HEAD_DIM ** -0.5
    swa_scale = SWA_HEAD_DIM ** -0.5

    zk = jnp.zeros((d, MLA_NOPE_DIM), F32)
    zp = jnp.zeros((d, MLA_HEAD_PAD - MLA_NOPE_DIM - MLA_ROPE_DIM), F32)
    w_kr = jnp.concatenate([zk, m_kr, zp, zk, _rot_cols(m_kr, MLA_ROPE_DIM), zp], axis=1)

    qu = mla_w_q_up[i].reshape(MLA_Q_RANK, MLA_HEADS, MLA_NOPE_DIM + MLA_ROPE_DIM)
    q_nope, q_rope = qu[..., :MLA_NOPE_DIM], qu[..., MLA_NOPE_DIM:]
    zq = jnp.zeros_like(q_rope)
    q_main = jnp.concatenate([q_nope, q_rope, zq], axis=-1)
    q_rot = jnp.concatenate([jnp.zeros_like(q_nope), _rot_cols(q_rope, MLA_ROPE_DIM), zq], axis=-1)
    w_q_up = jnp.concatenate([q_main.reshape(MLA_Q_RANK, -1), q_rot.reshape(MLA_Q_RANK, -1)], axis=1)

    kvu = mla_w_kv_up[i].reshape(MLA_KV_RANK, MLA_HEADS, MLA_NOPE_DIM + MLA_V_DIM)
    k_nope, v_up = kvu[..., :MLA_NOPE_DIM], kvu[..., MLA_NOPE_DIM:]
    zv = jnp.zeros_like(v_up)
    k_main = jnp.concatenate([k_nope, jnp.zeros_like(k_nope)], axis=-1)
    even = (jnp.arange(MLA_HEADS) % 2 == 0)[None, :, None]
    v_main = jnp.concatenate([jnp.where(even, v_up, zv), jnp.where(even, zv, v_up)], axis=-1)
    w_kv_up = jnp.concatenate([k_main.reshape(MLA_KV_RANK, -1), v_main.reshape(MLA_KV_RANK, -1)],
                              axis=1)

    wr = w_router[i].T
    wr_hi = wr.astype(BF16)
    wr_lo = (wr - wr_hi.astype(F32)).astype(BF16)
    wo = w_out[i]
    row = lambda g: g.reshape(1, -1).astype(F32)
    return {
        'g_attn': row(attn_norm[i]), 'g_q': row(mla_q_norm[i]), 'g_kv': row(mla_kv_norm[i]),
        'w_na': jnp.concatenate([na_q * na_scale, na_k, na_v], axis=1).astype(BF16),
        'w_mc': jnp.concatenate([m_cq, m_ckv], axis=1).astype(BF16),
        'w_kr': w_kr.astype(BF16),
        'w_swa': jnp.concatenate([s_q * swa_scale, _rot_cols(s_q, SWA_HEAD_DIM) * swa_scale,
                                  s_k, _rot_cols(s_k, SWA_HEAD_DIM), s_v], axis=1).astype(BF16),
        'w_q_up': w_q_up.astype(BF16), 'w_kv_up': w_kv_up.astype(BF16),
        'w_out_na': wo[:NA_WIDTH].astype(BF16),
        'w_out_mla': wo[NA_WIDTH:NA_WIDTH + MLA_WIDTH].astype(BF16),
        'w_out_swa': wo[NA_WIDTH + MLA_WIDTH:].astype(BF16),
        'g_ffn': row(ffn_norm[i]),
        'w_router': jnp.concatenate([wr_hi, wr_lo], axis=0),
        'w_gate': w_gate_e[i].astype(BF16), 'w_up': w_up_e[i].astype(BF16),
        'w_down': w_down_e[i].astype(BF16),
        'g_ple': row(ple_norm[i]),
        'w_ple_gate': w_ple_gate[i].astype(BF16), 'w_ple_proj': w_ple_proj[i].astype(BF16),
    }


def kernel(x, p, attn_norm, w_in, na_rpb, mla_q_norm, mla_w_q_up, mla_kv_norm, mla_w_kv_up,
           swa_sink, w_out, ffn_norm, w_router, w_gate_e, w_up_e, w_down_e, ple_norm, w_ple_gate,
           w_ple_proj, final_norm):
    b, seq, d = x.shape
    depth = w_in.shape[0]
    t = b * seq
    cap = EC_CAPACITY_FACTOR * seq // N_EXPERTS
    tabs = _rope_tables(seq)
    tri = jnp.triu(jnp.ones((LANE, LANE), F32)).astype(BF16)
    g_final = final_norm.reshape(1, -1).astype(F32)

    h = x.reshape(t, d)
    for i in range(depth):
        lw = _layer_weights(i, attn_norm, w_in, mla_q_norm, mla_w_q_up, mla_kv_norm, mla_w_kv_up,
                            w_out, ffn_norm, w_router, w_gate_e, w_up_e, w_down_e, ple_norm,
                            w_ple_gate, w_ple_proj)
        (na_q, na_k, na_v, m_q, m_k, m_v, s_q, s_k, s_v) = _inproj(h, lw, tabs, seq)
        r3 = lambda a: a.reshape(b, seq, a.shape[-1])
        o_na = _na_attention(r3(na_q), r3(na_k), r3(na_v), _na_bias_tables(na_rpb[i]))
        o_mla = _mla_attention(r3(m_q), r3(m_k), r3(m_v))
        o_swa = _swa_attention(r3(s_q), r3(s_k), r3(s_v), swa_sink[i].astype(F32))
        r2 = lambda a: a.reshape(t, a.shape[-1])
        h1, aff = _outproj(h, r2(o_na), r2(o_mla), r2(o_swa), lw, b, seq)
        idx, gate = _select(aff, tri, cap)
        idx = idx.reshape(b, N_EXPERTS, cap)
        h1b = h1.reshape(b, seq, d)
        y = _ffn(idx, h1b, gate, lw, cap)
        h2 = _scatter(idx, h1b, y, cap)
        h = _ple(h2.reshape(t, d), p[i].reshape(t, PLE_DIM), lw, g_final, i == depth - 1)
    return h.reshape(b, seq, d)
```

```python
import functools
import math

import jax
import jax.numpy as jnp
from jax import lax
from jax.experimental import pallas as pl
from jax.experimental.pallas import tpu as pltpu

F32 = jnp.float32
BF16 = jnp.bfloat16

GRID_W = 64
NA_HEADS = 4
NA_HEAD_DIM = 64
NA_WIN_ROWS = 8
NA_WIN_COLS = 16
MLA_HEADS = 8
MLA_Q_RANK = 384
MLA_KV_RANK = 256
MLA_NOPE_DIM = 64
MLA_ROPE_DIM = 32
MLA_V_DIM = 64
SWA_HEADS = 4
SWA_KV_HEADS = 2
SWA_HEAD_DIM = 64
SWA_WINDOW = 128
SWA_BLOCK = 128
N_EXPERTS = 16
EXPERT_FF = 512
EC_CAPACITY_FACTOR = 2
PLE_DIM = 256
ROPE_THETA = 10000.0
NORM_EPS = 1e-6
NEG_INF = -1e30

NA_WIDTH = NA_HEADS * NA_HEAD_DIM
MLA_WIDTH = MLA_HEADS * MLA_V_DIM
SWA_WIDTH = SWA_HEADS * SWA_HEAD_DIM
SWA_KV_WIDTH = SWA_KV_HEADS * SWA_HEAD_DIM
IN_SIZES = (NA_WIDTH, NA_WIDTH, NA_WIDTH, MLA_Q_RANK, MLA_KV_RANK, MLA_ROPE_DIM,
            SWA_WIDTH, SWA_KV_WIDTH, SWA_KV_WIDTH)

LANE = 128
MLA_HEAD_PAD = LANE
VMEM_LIMIT = 56 * 1024 * 1024

TOK_TILE = 512
NA_QROWS = 4
NA_KROWS = 12
MLA_TQ = 256
MLA_HEADS_PER_STEP = 8


def _cparams(sem):
    return pltpu.CompilerParams(dimension_semantics=sem, vmem_limit_bytes=VMEM_LIMIT)


def _dot(a, b):
    return jnp.dot(a, b, preferred_element_type=F32)


def _dot_t(a, b):
    return lax.dot_general(a, b, (((1,), (1,)), ((), ())), preferred_element_type=F32)


def _rms(x, g):
    return x * lax.rsqrt(jnp.mean(x * x, axis=-1, keepdims=True) + NORM_EPS) * g


def _inproj_kernel(h_ref, g_ref, gq_ref, gkv_ref, wna_ref, wmc_ref, wkr_ref, wswa_ref,
                   wq_ref, wkv_ref, cq_ref, sq_ref, ck_ref, sk_ref, cs_ref, ss_ref,
                   naq_ref, nak_ref, nav_ref, mq_ref, mk_ref, mv_ref,
                   swq_ref, swk_ref, swv_ref):
    xn = _rms(h_ref[...], g_ref[...]).astype(BF16)

    pna = _dot(xn, wna_ref[...])
    naq_ref[...] = pna[:, :NA_WIDTH].astype(BF16)
    nak_ref[...] = pna[:, NA_WIDTH:2 * NA_WIDTH].astype(BF16)
    nav_ref[...] = pna[:, 2 * NA_WIDTH:].astype(BF16)

    pmc = _dot(xn, wmc_ref[...])
    cqn = _rms(pmc[:, :MLA_Q_RANK], gq_ref[...]).astype(BF16)
    ckvn = _rms(pmc[:, MLA_Q_RANK:], gkv_ref[...]).astype(BF16)
    q2 = _dot(cqn, wq_ref[...])
    kv2 = _dot(ckvn, wkv_ref[...])
    pkr = _dot(xn, wkr_ref[...])
    kr = pkr[:, :LANE] * ck_ref[...] + pkr[:, LANE:] * sk_ref[...]
    hw = MLA_HEADS * MLA_HEAD_PAD
    cq = cq_ref[...]
    sq = sq_ref[...]
    for hd in range(MLA_HEADS):
        lo, hi = hd * MLA_HEAD_PAD, (hd + 1) * MLA_HEAD_PAD
        mq_ref[:, lo:hi] = (q2[:, lo:hi] * cq + q2[:, hw + lo:hw + hi] * sq).astype(BF16)
        mk_ref[:, lo:hi] = (kv2[:, lo:hi] + kr).astype(BF16)
    mv_ref[...] = kv2[:, hw:].astype(BF16)

    ps = _dot(xn, wswa_ref[...])
    cs = cs_ref[...]
    ss = ss_ref[...]
    for j in range(SWA_WIDTH // LANE):
        lo, hi = j * LANE, (j + 1) * LANE
        swq_ref[:, lo:hi] = (ps[:, lo:hi] * cs
                             + ps[:, SWA_WIDTH + lo:SWA_WIDTH + hi] * ss).astype(BF16)
    o = 2 * SWA_WIDTH
    swk_ref[...] = (ps[:, o:o + SWA_KV_WIDTH] * cs
                    + ps[:, o + SWA_KV_WIDTH:o + 2 * SWA_KV_WIDTH] * ss).astype(BF16)
    swv_ref[...] = ps[:, o + 2 * SWA_KV_WIDTH:].astype(BF16)


def _inproj(h, lw, tabs, seq):
    t, d = h.shape
    tm = TOK_TILE
    nt = t // tm
    spt = seq // tm

    def full(a):
        return pl.BlockSpec(a.shape, lambda i: (0,) * a.ndim)

    def tab(a):
        return pl.BlockSpec((tm, a.shape[1]), lambda i: (i % spt, 0))

    def tok(w):
        return pl.BlockSpec((tm, w), lambda i: (i, 0))

    weights = [lw['g_attn'], lw['g_q'], lw['g_kv'], lw['w_na'], lw['w_mc'], lw['w_kr'],
               lw['w_swa'], lw['w_q_up'], lw['w_kv_up']]
    tables = [tabs['cq'], tabs['sq'], tabs['ck'], tabs['sk'], tabs['cs'], tabs['ss']]
    widths = [NA_WIDTH, NA_WIDTH, NA_WIDTH, MLA_HEADS * MLA_HEAD_PAD, MLA_HEADS * MLA_HEAD_PAD,
              MLA_HEADS * MLA_HEAD_PAD, SWA_WIDTH, SWA_KV_WIDTH, SWA_KV_WIDTH]
    return pl.pallas_call(
        _inproj_kernel,
        grid=(nt,),
        in_specs=[tok(d)] + [full(w) for w in weights] + [tab(a) for a in tables],
        out_specs=[tok(w) for w in widths],
        out_shape=[jax.ShapeDtypeStruct((t, w), BF16) for w in widths],
        compiler_params=_cparams(("parallel",)),
        name="inproj",
    )(h, *weights, *tables)


def _na_kernel(q_ref, k_ref, v_ref, tbl_ref, o_ref, *, n_blocks):
    blk = pl.program_id(1)
    last_start = (n_blocks * NA_QROWS - NA_KROWS)
    ks = jnp.clip(blk * NA_QROWS - NA_WIN_ROWS // 2, 0, last_start) * GRID_W
    ks = pl.multiple_of(ks, GRID_W)
    nk = NA_KROWS * GRID_W
    outs = []
    for hd in range(NA_HEADS):
        lo, hi = hd * NA_HEAD_DIM, (hd + 1) * NA_HEAD_DIM
        q = q_ref[0, :, lo:hi]
        k = k_ref[0, pl.ds(ks, nk), lo:hi]
        v = v_ref[0, pl.ds(ks, nk), lo:hi]
        s = _dot_t(q, k) + tbl_ref[0, hd]
        m = jnp.max(s, axis=-1, keepdims=True)
        e = jnp.exp(s - m)
        l = jnp.sum(e, axis=-1, keepdims=True)
        outs.append(_dot(e.astype(BF16), v) / l)
    o_ref[0] = jnp.concatenate(outs, axis=-1).astype(BF16)


def _na_attention(q, k, v, tbl):
    b, s, w = q.shape
    rows = s // GRID_W
    nb = rows // NA_QROWS
    tq = NA_QROWS * GRID_W

    def tbl_map(bi, blk):
        sel = jnp.where(blk == 0, 0, jnp.where(blk == nb - 1, 2, 1))
        return (sel, 0, 0, 0)

    return pl.pallas_call(
        functools.partial(_na_kernel, n_blocks=nb),
        grid=(b, nb),
        in_specs=[pl.BlockSpec((1, tq, w), lambda bi, blk: (bi, blk, 0)),
                  pl.BlockSpec((1, s, w), lambda bi, blk: (bi, 0, 0)),
                  pl.BlockSpec((1, s, w), lambda bi, blk: (bi, 0, 0)),
                  pl.BlockSpec((1,) + tbl.shape[1:], tbl_map)],
        out_specs=pl.BlockSpec((1, tq, w), lambda bi, blk: (bi, blk, 0)),
        out_shape=jax.ShapeDtypeStruct((b, s, w), BF16),
        compiler_params=_cparams(("parallel", "arbitrary")),
        name="na_attn",
    )(q, k, v, tbl)


def _mla_kernel(q_ref, k_ref, v_ref, o_ref):
    for pair in range(MLA_HEADS_PER_STEP // 2):
        acc = None
        for hh in range(2 * pair, 2 * pair + 2):
            lo, hi = hh * MLA_HEAD_PAD, (hh + 1) * MLA_HEAD_PAD
            q = q_ref[0, :, lo:hi]
            k = k_ref[0, :, lo:hi]
            v = v_ref[0, :, lo:hi]
            s = _dot_t(q, k)
            m = jnp.max(s, axis=-1, keepdims=True)
            e = jnp.exp2(s - m)
            l = jnp.sum(e, axis=-1, keepdims=True)
            o = _dot(e.astype(BF16), v) / l
            acc = o if acc is None else acc + o
        o_ref[0, :, pair * LANE:(pair + 1) * LANE] = acc.astype(BF16)


def _mla_attention(q, k, v):
    b, s, _ = q.shape
    hps = MLA_HEADS_PER_STEP
    pw = hps * MLA_HEAD_PAD
    ow = hps * MLA_V_DIM
    ngrp = MLA_HEADS // hps
    nq = s // MLA_TQ
    return pl.pallas_call(
        _mla_kernel,
        grid=(b, ngrp, nq),
        in_specs=[pl.BlockSpec((1, MLA_TQ, pw), lambda bi, hg, qi: (bi, qi, hg)),
                  pl.BlockSpec((1, s, pw), lambda bi, hg, qi: (bi, 0, hg)),
                  pl.BlockSpec((1, s, pw), lambda bi, hg, qi: (bi, 0, hg))],
        out_specs=pl.BlockSpec((1, MLA_TQ, ow), lambda bi, hg, qi: (bi, qi, hg)),
        out_shape=jax.ShapeDtypeStruct((b, s, MLA_WIDTH), BF16),
        compiler_params=_cparams(("parallel", "parallel", "arbitrary")),
        name="mla_attn",
    )(q, k, v)


def _swa_kernel(sink_ref, q_ref, k_ref, v_ref, o_ref, *, seq):
    n = pl.program_id(1)
    blk = SWA_BLOCK
    nk = 3 * blk
    ks = pl.multiple_of(jnp.clip((n - 1) * blk, 0, seq - nk), blk)
    group = SWA_HEADS // SWA_KV_HEADS
    rows = group * blk
    ri = lax.broadcasted_iota(jnp.int32, (rows, nk), 0)
    ci = lax.broadcasted_iota(jnp.int32, (rows, nk), 1)
    rel = (ks + ci) - (n * blk + (ri & (blk - 1)))
    valid = jnp.abs(rel) <= SWA_WINDOW
    rcol = lax.broadcasted_iota(jnp.int32, (rows, 1), 0)
    outs = []
    for kvh in range(SWA_KV_HEADS):
        qs = []
        sink = jnp.zeros((rows, 1), F32)
        for g in range(group):
            hd = kvh * group + g
            qs.append(q_ref[0, :, hd * SWA_HEAD_DIM:(hd + 1) * SWA_HEAD_DIM])
            sink = jnp.where((rcol >= g * blk) & (rcol < (g + 1) * blk), sink_ref[hd], sink)
        q = jnp.concatenate(qs, axis=0)
        lo, hi = kvh * SWA_HEAD_DIM, (kvh + 1) * SWA_HEAD_DIM
        k = k_ref[0, pl.ds(ks, nk), lo:hi]
        v = v_ref[0, pl.ds(ks, nk), lo:hi]
        s = jnp.where(valid, _dot_t(q, k), NEG_INF)
        m = jnp.maximum(jnp.max(s, axis=-1, keepdims=True), sink)
        e = jnp.exp(s - m)
        den = jnp.sum(e, axis=-1, keepdims=True) + jnp.exp(sink - m)
        o = _dot(e.astype(BF16), v) / den
        for g in range(group):
            outs.append(o[g * blk:(g + 1) * blk])
    o_ref[0] = jnp.concatenate(outs, axis=-1).astype(BF16)


def _swa_attention(q, k, v, sink):
    b, s, w = q.shape
    kw = k.shape[2]
    nb = s // SWA_BLOCK
    return pl.pallas_call(
        functools.partial(_swa_kernel, seq=s),
        grid=(b, nb),
        in_specs=[pl.BlockSpec(memory_space=pltpu.SMEM),
                  pl.BlockSpec((1, SWA_BLOCK, w), lambda bi, n: (bi, n, 0)),
                  pl.BlockSpec((1, s, kw), lambda bi, n: (bi, 0, 0)),
                  pl.BlockSpec((1, s, kw), lambda bi, n: (bi, 0, 0))],
        out_specs=pl.BlockSpec((1, SWA_BLOCK, w), lambda bi, n: (bi, n, 0)),
        out_shape=jax.ShapeDtypeStruct((b, s, w), BF16),
        compiler_params=_cparams(("parallel", "arbitrary")),
        name="swa_attn",
    )(sink, q, k, v)


def _outproj_kernel(h_ref, ona_ref, omla_ref, oswa_ref, wna_ref, wmla_ref, wswa_ref,
                    g_ref, wr_ref, h1_ref, aff_ref):
    h1 = (h_ref[...] + _dot(ona_ref[...], wna_ref[...]) + _dot(omla_ref[...], wmla_ref[...])
          + _dot(oswa_ref[...], wswa_ref[...]))
    h1_ref[...] = h1
    hn = _rms(h1, g_ref[...])
    hn_hi = hn.astype(BF16)
    hn_lo = (hn - hn_hi.astype(F32)).astype(BF16)
    wr = wr_ref[...]
    p_hi = _dot_t(wr, hn_hi)
    p_lo = _dot_t(wr, hn_lo)
    logits = p_hi[:N_EXPERTS] + p_hi[N_EXPERTS:] + p_lo[:N_EXPERTS]
    m = jnp.max(logits, axis=0, keepdims=True)
    e = jnp.exp(logits - m)
    aff_ref[0] = e / jnp.sum(e, axis=0, keepdims=True)


def _outproj(h, o_na, o_mla, o_swa, lw, batch, seq):
    t, d = h.shape
    tm = TOK_TILE
    spt = seq // tm

    def full(a):
        return pl.BlockSpec(a.shape, lambda i: (0,) * a.ndim)

    def tok(w):
        return pl.BlockSpec((tm, w), lambda i: (i, 0))

    weights = [lw['w_out_na'], lw['w_out_mla'], lw['w_out_swa'], lw['g_ffn'], lw['w_router']]
    return pl.pallas_call(
        _outproj_kernel,
        grid=(t // tm,),
        in_specs=[tok(d), tok(NA_WIDTH), tok(MLA_WIDTH), tok(SWA_WIDTH)] + [full(w) for w in weights],
        out_specs=[tok(d), pl.BlockSpec((1, N_EXPERTS, tm), lambda i: (i // spt, 0, i % spt))],
        out_shape=[jax.ShapeDtypeStruct((t, d), F32),
                   jax.ShapeDtypeStruct((batch, N_EXPERTS, seq), F32)],
        compiler_params=_cparams(("parallel",)),
        name="outproj_router",
    )(h, o_na, o_mla, o_swa, *weights)


def _select_kernel(aff_ref, tri_ref, idx_ref, csum_scr, *, cap):
    x = aff_ref[0]
    ne, seq = x.shape
    capf = float(cap)

    def count_ge(v):
        return jnp.sum(jnp.where(x >= v, 1.0, 0.0), axis=1, keepdims=True)

    def search(i, tb):
        cand = tb | jnp.left_shift(jnp.int32(1), 30 - i)
        return jnp.where(count_ge(pltpu.bitcast(cand, F32)) >= capf, cand, tb)

    tb = lax.fori_loop(0, 31, search, jnp.zeros((ne, 1), jnp.int32))

    def refine(i, lohi):
        lo, hi = lohi
        mid = 0.5 * (lo + hi)
        ok = count_ge(mid) >= capf
        return jnp.where(ok, mid, lo), jnp.where(ok, hi, mid)

    thr, _ = lax.fori_loop(0, 24, refine, (pltpu.bitcast(tb, F32), pltpu.bitcast(tb + 1, F32)))
    gt = jnp.where(x > thr, 1.0, 0.0)
    eq = jnp.where(x == thr, 1.0, 0.0)
    need = capf - jnp.sum(gt, axis=1, keepdims=True)
    tri = tri_ref[...]

    def cumsum(mask):
        outs = []
        off = jnp.zeros((ne, 1), F32)
        for c in range(seq // LANE):
            cs = _dot(mask[:, c * LANE:(c + 1) * LANE].astype(BF16), tri) + off
            outs.append(cs)
            off = cs[:, LANE - 1:LANE]
        return jnp.concatenate(outs, axis=1)

    tie_rank = cumsum(eq) - eq
    sel = gt + eq * jnp.where(tie_rank < need, 1.0, 0.0)
    csum = cumsum(sel)
    for e in range(ne):
        csum_scr[e] = csum[e:e + 1, :]

    n_chunks = seq // LANE
    slot_blocks = cap // LANE

    def per_expert(e, carry):
        for sb in range(slot_blocks):
            slot = (lax.broadcasted_iota(jnp.int32, (LANE, LANE), 0) + sb * LANE).astype(F32)

            def chunk(c, acc):
                off = pl.multiple_of(c * LANE, LANE)
                return acc + jnp.where(csum_scr[e, :, pl.ds(off, LANE)] <= slot, 1.0, 0.0)

            acc = lax.fori_loop(0, n_chunks, chunk, jnp.zeros((LANE, LANE), F32), unroll=4)
            idx_ref[0, e, sb * LANE:(sb + 1) * LANE, :] = (
                jnp.sum(acc, axis=1, keepdims=True).astype(jnp.int32))
        return carry

    lax.fori_loop(0, ne, per_expert, 0)


def _select(aff, tri, cap):
    b, ne, seq = aff.shape
    return pl.pallas_call(
        functools.partial(_select_kernel, cap=cap),
        grid=(b,),
        in_specs=[pl.BlockSpec((1, ne, seq), lambda bi: (bi, 0, 0)),
                  pl.BlockSpec(tri.shape, lambda bi: (0, 0))],
        out_specs=pl.BlockSpec((1, ne, cap, 1), lambda bi: (bi, 0, 0, 0)),
        out_shape=jax.ShapeDtypeStruct((b, ne, cap, 1), jnp.int32),
        scratch_shapes=[pltpu.VMEM((ne, 1, seq), F32)],
        compiler_params=_cparams(("parallel",)),
        name="ec_select",
    )(aff, tri)


def _ffn_kernel(idx_ref, h_ref, g_ref, wg_ref, wu_ref, wd_ref, y_ref, xs_scr, *, cap):
    def gather(i, carry):
        t = idx_ref[0, 0, i]
        xs_scr[pl.ds(i, 1), :] = h_ref[0, pl.ds(t, 1), :]
        return carry

    lax.fori_loop(0, cap, gather, 0, unroll=8)
    xn = _rms(xs_scr[...], g_ref[...]).astype(BF16)
    a = _dot(xn, wg_ref[0])
    u = _dot(xn, wu_ref[0])
    mid = (a * jax.nn.sigmoid(a) * u).astype(BF16)
    y_ref[0, 0] = _dot(mid, wd_ref[0])


def _ffn(idx, h1, lw, cap):
    b, seq, d = h1.shape
    ne = N_EXPERTS
    ff = EXPERT_FF
    idx3 = idx.reshape(b * ne, 1, cap)
    return pl.pallas_call(
        functools.partial(_ffn_kernel, cap=cap),
        grid=(b, ne),
        in_specs=[pl.BlockSpec((1, 1, cap), lambda bi, e: (bi * ne + e, 0, 0),
                               memory_space=pltpu.SMEM),
                  pl.BlockSpec((1, seq, d), lambda bi, e: (bi, 0, 0)),
                  pl.BlockSpec((1, d), lambda bi, e: (0, 0)),
                  pl.BlockSpec((1, d, ff), lambda bi, e: (e, 0, 0)),
                  pl.BlockSpec((1, d, ff), lambda bi, e: (e, 0, 0)),
                  pl.BlockSpec((1, ff, d), lambda bi, e: (e, 0, 0))],
        out_specs=pl.BlockSpec((1, 1, cap, d), lambda bi, e: (bi, e, 0, 0)),
        out_shape=jax.ShapeDtypeStruct((b, ne, cap, d), F32),
        scratch_shapes=[pltpu.VMEM((cap, d), F32)],
        compiler_params=_cparams(("parallel", "arbitrary")),
        name="ec_ffn",
    )(idx3, h1, lw['g_ffn'], lw['w_gate'], lw['w_up'], lw['w_down'])


SCATTER_ROWS = 8


def _scatter_kernel(idx_ref, aff_ref, y_ref, o_ref, *, cap):
    @pl.when(pl.program_id(1) == 0)
    def _():
        o_ref[...] = jnp.zeros_like(o_ref)

    def group(j, carry):
        base = pl.multiple_of(j * SCATTER_ROWS, SCATTER_ROWS)
        toks = [idx_ref[0, 0, base + r] for r in range(SCATTER_ROWS)]
        rows = [o_ref[0, pl.ds(toks[r], 1), :] for r in range(SCATTER_ROWS)]
        for r in range(SCATTER_ROWS):
            gate = aff_ref[0, 0, toks[r]]
            o_ref[0, pl.ds(toks[r], 1), :] = rows[r] + gate * y_ref[0, 0, pl.ds(base + r, 1), :]
        return carry

    lax.fori_loop(0, cap // SCATTER_ROWS, group, 0)


def _scatter(idx, aff, y, seq, cap):
    b, ne, _, d = y.shape
    idx3 = idx.reshape(b * ne, 1, cap)
    aff3 = aff.reshape(b * ne, 1, seq)
    return pl.pallas_call(
        functools.partial(_scatter_kernel, cap=cap),
        grid=(b, ne),
        in_specs=[pl.BlockSpec((1, 1, cap), lambda bi, e: (bi * ne + e, 0, 0),
                               memory_space=pltpu.SMEM),
                  pl.BlockSpec((1, 1, seq), lambda bi, e: (bi * ne + e, 0, 0),
                               memory_space=pltpu.SMEM),
                  pl.BlockSpec((1, 1, cap, d), lambda bi, e: (bi, e, 0, 0))],
        out_specs=pl.BlockSpec((1, seq, d), lambda bi, e: (bi, 0, 0)),
        out_shape=jax.ShapeDtypeStruct((b, seq, d), F32),
        compiler_params=_cparams(("parallel", "arbitrary")),
        name="ec_scatter",
    )(idx3, aff3, y)


def _ple_kernel(h_ref, moe_ref, p_ref, g_ref, wg_ref, wp_ref, gf_ref, o_ref, *, final):
    h = h_ref[...] + moe_ref[...]
    gate = jax.nn.sigmoid(_dot(_rms(h, g_ref[...]).astype(BF16), wg_ref[...]))
    out = h + gate * _dot(p_ref[...].astype(BF16), wp_ref[...])
    if final:
        out = _rms(out, gf_ref[...])
    o_ref[...] = out


def _ple(h, moe, p, lw, g_final, final):
    t, d = h.shape
    tm = TOK_TILE

    def full(a):
        return pl.BlockSpec(a.shape, lambda i: (0,) * a.ndim)

    def tok(w):
        return pl.BlockSpec((tm, w), lambda i: (i, 0))

    weights = [lw['g_ple'], lw['w_ple_gate'], lw['w_ple_proj'], g_final]
    return pl.pallas_call(
        functools.partial(_ple_kernel, final=final),
        grid=(t // tm,),
        in_specs=[tok(d), tok(d), tok(p.shape[1])] + [full(w) for w in weights],
        out_specs=tok(d),
        out_shape=jax.ShapeDtypeStruct((t, d), F32),
        compiler_params=_cparams(("parallel",)),
        name="ple",
    )(h, moe, p, *weights)


def _rot_cols(w, d):
    g = w.reshape(w.shape[:-1] + (w.shape[-1] // d, d))
    return jnp.concatenate([-g[..., d // 2:], g[..., :d // 2]], axis=-1).reshape(w.shape)


def _rope_tables(seq):
    pos = jnp.arange(seq, dtype=F32)

    def cs(dim):
        inv = 1.0 / (ROPE_THETA ** (jnp.arange(0, dim, 2, dtype=F32) / dim))
        ang = pos[:, None] * inv[None, :]
        return jnp.cos(ang), jnp.sin(ang)

    cm, sm = cs(MLA_ROPE_DIM)
    cw, sw = cs(SWA_HEAD_DIM)
    z = lambda n: jnp.zeros((seq, n), F32)
    one = jnp.ones((seq, MLA_NOPE_DIM), F32)
    pad = MLA_HEAD_PAD - MLA_NOPE_DIM - MLA_ROPE_DIM
    qs = (MLA_NOPE_DIM + MLA_ROPE_DIM) ** -0.5 * math.log2(math.e)
    return {
        'cq': qs * jnp.concatenate([one, cm, cm, z(pad)], axis=1),
        'sq': qs * jnp.concatenate([z(MLA_NOPE_DIM), sm, sm, z(pad)], axis=1),
        'ck': jnp.concatenate([z(MLA_NOPE_DIM), cm, cm, z(pad)], axis=1),
        'sk': jnp.concatenate([z(MLA_NOPE_DIM), sm, sm, z(pad)], axis=1),
        'cs': jnp.concatenate([cw, cw] * (LANE // SWA_HEAD_DIM), axis=1),
        'ss': jnp.concatenate([sw, sw] * (LANE // SWA_HEAD_DIM), axis=1),
    }


def _na_bias_tables(rpb):
    nh = rpb.shape[0]
    qi = jnp.arange(NA_QROWS)[:, None]
    ki = jnp.arange(NA_KROWS)[None, :]
    half = NA_WIN_ROWS // 2
    cases = [
        ((ki < NA_WIN_ROWS) & (qi >= 0), ki - qi),
        ((ki >= qi) & (ki < qi + NA_WIN_ROWS), ki - half - qi),
        ((ki >= NA_KROWS - NA_WIN_ROWS) & (qi >= 0), ki - (NA_KROWS - NA_QROWS) - qi),
    ]
    row_ok = jnp.stack([ok for ok, _ in cases])
    dri = jnp.stack([jnp.clip(dr + NA_WIN_ROWS - 1, 0, 2 * NA_WIN_ROWS - 2) for _, dr in cases])
    c = jnp.arange(GRID_W)[:, None]
    kc = jnp.arange(GRID_W)[None, :]
    c0 = jnp.clip(c - NA_WIN_COLS // 2, 0, GRID_W - NA_WIN_COLS)
    col_ok = (kc >= c0) & (kc < c0 + NA_WIN_COLS)
    dci = jnp.clip(kc - c, -(NA_WIN_COLS - 1), NA_WIN_COLS - 1) + NA_WIN_COLS - 1
    one_r = jax.nn.one_hot(dri, 2 * NA_WIN_ROWS - 1, dtype=F32)
    one_c = jax.nn.one_hot(dci, 2 * NA_WIN_COLS - 1, dtype=F32)
    bias = jnp.einsum('xqka,hab,cnb->xhqckn', one_r, rpb.astype(F32), one_c,
                      precision=lax.Precision.HIGHEST)
    ok = row_ok[:, None, :, None, :, None] & col_ok[None, None, None, :, None, :]
    tab = jnp.where(ok, bias, NEG_INF)
    return tab.reshape(3, nh, NA_QROWS * GRID_W, NA_KROWS * GRID_W)


def _layer_weights(i, attn_norm, w_in, mla_q_norm, mla_w_q_up, mla_kv_norm, mla_w_kv_up, w_out,
                   ffn_norm, w_router, w_gate_e, w_up_e, w_down_e, ple_norm, w_ple_gate,
                   w_ple_proj):
    offs = [0]
    for n in IN_SIZES:
        offs.append(offs[-1] + n)
    parts = [w_in[i][:, offs[j]:offs[j + 1]] for j in range(len(IN_SIZES))]
    na_q, na_k, na_v, m_cq, m_ckv, m_kr, s_q, s_k, s_v = parts
    d = w_in.shape[1]
    na_scale = NA_HEAD_DIM ** -0.5
    swa_scale = SWA_HEAD_DIM ** -0.5

    zk = jnp.zeros((d, MLA_NOPE_DIM), F32)
    zp = jnp.zeros((d, MLA_HEAD_PAD - MLA_NOPE_DIM - MLA_ROPE_DIM), F32)
    w_kr = jnp.concatenate([zk, m_kr, zp, zk, _rot_cols(m_kr, MLA_ROPE_DIM), zp], axis=1)

    qu = mla_w_q_up[i].reshape(MLA_Q_RANK, MLA_HEADS, MLA_NOPE_DIM + MLA_ROPE_DIM)
    q_nope, q_rope = qu[..., :MLA_NOPE_DIM], qu[..., MLA_NOPE_DIM:]
    zq = jnp.zeros_like(q_rope)
    q_main = jnp.concatenate([q_nope, q_rope, zq], axis=-1)
    q_rot = jnp.concatenate([jnp.zeros_like(q_nope), _rot_cols(q_rope, MLA_ROPE_DIM), zq], axis=-1)
    w_q_up = jnp.concatenate([q_main.reshape(MLA_Q_RANK, -1), q_rot.reshape(MLA_Q_RANK, -1)], axis=1)

    kvu = mla_w_kv_up[i].reshape(MLA_KV_RANK, MLA_HEADS, MLA_NOPE_DIM + MLA_V_DIM)
    k_nope, v_up = kvu[..., :MLA_NOPE_DIM], kvu[..., MLA_NOPE_DIM:]
    zv = jnp.zeros_like(v_up)
    k_main = jnp.concatenate([k_nope, jnp.zeros_like(k_nope)], axis=-1)
    even = (jnp.arange(MLA_HEADS) % 2 == 0)[None, :, None]
    v_main = jnp.concatenate([jnp.where(even, v_up, zv), jnp.where(even, zv, v_up)], axis=-1)
    w_kv_up = jnp.concatenate([k_main.reshape(MLA_KV_RANK, -1), v_main.reshape(MLA_KV_RANK, -1)],
                              axis=1)

    wr = w_router[i].T
    wr_hi = wr.astype(BF16)
    wr_lo = (wr - wr_hi.astype(F32)).astype(BF16)
    wo = w_out[i]
    row = lambda g: g.reshape(1, -1).astype(F32)
    return {
        'g_attn': row(attn_norm[i]), 'g_q': row(mla_q_norm[i]), 'g_kv': row(mla_kv_norm[i]),
        'w_na': jnp.concatenate([na_q * na_scale, na_k, na_v], axis=1).astype(BF16),
        'w_mc': jnp.concatenate([m_cq, m_ckv], axis=1).astype(BF16),
        'w_kr': w_kr.astype(BF16),
        'w_swa': jnp.concatenate([s_q * swa_scale, _rot_cols(s_q, SWA_HEAD_DIM) * swa_scale,
                                  s_k, _rot_cols(s_k, SWA_HEAD_DIM), s_v], axis=1).astype(BF16),
        'w_q_up': w_q_up.astype(BF16), 'w_kv_up': w_kv_up.astype(BF16),
        'w_out_na': wo[:NA_WIDTH].astype(BF16),
        'w_out_mla': wo[NA_WIDTH:NA_WIDTH + MLA_WIDTH].astype(BF16),
        'w_out_swa': wo[NA_WIDTH + MLA_WIDTH:].astype(BF16),
        'g_ffn': row(ffn_norm[i]),
        'w_router': jnp.concatenate([wr_hi, wr_lo], axis=0),
        'w_gate': w_gate_e[i].astype(BF16), 'w_up': w_up_e[i].astype(BF16),
        'w_down': w_down_e[i].astype(BF16),
        'g_ple': row(ple_norm[i]),
        'w_ple_gate': w_ple_gate[i].astype(BF16), 'w_ple_proj': w_ple_proj[i].astype(BF16),
    }


def kernel(x, p, attn_norm, w_in, na_rpb, mla_q_norm, mla_w_q_up, mla_kv_norm, mla_w_kv_up,
           swa_sink, w_out, ffn_norm, w_router, w_gate_e, w_up_e, w_down_e, ple_norm, w_ple_gate,
           w_ple_proj, final_norm):
    b, seq, d = x.shape
    depth = w_in.shape[0]
    t = b * seq
    cap = EC_CAPACITY_FACTOR * seq // N_EXPERTS
    tabs = _rope_tables(seq)
    tri = jnp.triu(jnp.ones((LANE, LANE), F32)).astype(BF16)
    g_final = final_norm.reshape(1, -1).astype(F32)

    h = x.reshape(t, d)
    for i in range(depth):
        lw = _layer_weights(i, attn_norm, w_in, mla_q_norm, mla_w_q_up, mla_kv_norm, mla_w_kv_up,
                            w_out, ffn_norm, w_router, w_gate_e, w_up_e, w_down_e, ple_norm,
                            w_ple_gate, w_ple_proj)
        (na_q, na_k, na_v, m_q, m_k, m_v, s_q, s_k, s_v) = _inproj(h, lw, tabs, seq)
        r3 = lambda a: a.reshape(b, seq, a.shape[-1])
        o_na = _na_attention(r3(na_q), r3(na_k), r3(na_v), _na_bias_tables(na_rpb[i]))
        o_mla = _mla_attention(r3(m_q), r3(m_k), r3(m_v))
        o_swa = _swa_attention(r3(s_q), r3(s_k), r3(s_v), swa_sink[i].astype(F32))
        r2 = lambda a: a.reshape(t, a.shape[-1])
        h1, aff = _outproj(h, r2(o_na), r2(o_mla), r2(o_swa), lw, b, seq)
        idx = _select(aff, tri, cap).reshape(b, N_EXPERTS, cap)
        y = _ffn(idx, h1.reshape(b, seq, d), lw, cap)
        moe = _scatter(idx, aff, y, seq, cap)
        h = _ple(h1, moe.reshape(t, d), p[i].reshape(t, PLE_DIM), lw, g_final, i == depth - 1)
    return h.reshape(b, seq, d)
```

```python
import functools
import math

import jax
import jax.numpy as jnp
from jax import lax
from jax.experimental import pallas as pl
from jax.experimental.pallas import tpu as pltpu

F32 = jnp.float32
BF16 = jnp.bfloat16

GRID_W = 64
NA_HEADS = 4
NA_HEAD_DIM = 64
NA_WIN_ROWS = 8
NA_WIN_COLS = 16
MLA_HEADS = 8
MLA_Q_RANK = 384
MLA_KV_RANK = 256
MLA_NOPE_DIM = 64
MLA_ROPE_DIM = 32
MLA_V_DIM = 64
SWA_HEADS = 4
SWA_KV_HEADS = 2
SWA_HEAD_DIM = 64
SWA_WINDOW = 128
SWA_BLOCK = 128
N_EXPERTS = 16
EXPERT_FF = 512
EC_CAPACITY_FACTOR = 2
PLE_DIM = 256
ROPE_THETA = 10000.0
NORM_EPS = 1e-6
NEG_INF = -1e30

NA_WIDTH = NA_HEADS * NA_HEAD_DIM
MLA_WIDTH = MLA_HEADS * MLA_V_DIM
SWA_WIDTH = SWA_HEADS * SWA_HEAD_DIM
SWA_KV_WIDTH = SWA_KV_HEADS * SWA_HEAD_DIM
IN_SIZES = (NA_WIDTH, NA_WIDTH, NA_WIDTH, MLA_Q_RANK, MLA_KV_RANK, MLA_ROPE_DIM,
            SWA_WIDTH, SWA_KV_WIDTH, SWA_KV_WIDTH)

LANE = 128
MLA_HEAD_PAD = LANE
VMEM_LIMIT = 56 * 1024 * 1024

TOK_TILE = 512
NA_QROWS = 4
NA_KROWS = 12
MLA_TQ = 256
MLA_HEADS_PER_STEP = 8


def _cparams(sem):
    return pltpu.CompilerParams(dimension_semantics=sem, vmem_limit_bytes=VMEM_LIMIT)


def _dot(a, b):
    return jnp.dot(a, b, preferred_element_type=F32)


def _dot_t(a, b):
    return lax.dot_general(a, b, (((1,), (1,)), ((), ())), preferred_element_type=F32)


def _rms(x, g):
    return x * lax.rsqrt(jnp.mean(x * x, axis=-1, keepdims=True) + NORM_EPS) * g


def _inproj_kernel(h_ref, g_ref, gq_ref, gkv_ref, wna_ref, wmc_ref, wkr_ref, wswa_ref,
                   wq_ref, wkv_ref, cq_ref, sq_ref, ck_ref, sk_ref, cs_ref, ss_ref,
                   naq_ref, nak_ref, nav_ref, mq_ref, mk_ref, mv_ref,
                   swq_ref, swk_ref, swv_ref):
    xn = _rms(h_ref[...], g_ref[...]).astype(BF16)

    pna = _dot(xn, wna_ref[...])
    naq_ref[...] = pna[:, :NA_WIDTH].astype(BF16)
    nak_ref[...] = pna[:, NA_WIDTH:2 * NA_WIDTH].astype(BF16)
    nav_ref[...] = pna[:, 2 * NA_WIDTH:].astype(BF16)

    pmc = _dot(xn, wmc_ref[...])
    cqn = _rms(pmc[:, :MLA_Q_RANK], gq_ref[...]).astype(BF16)
    ckvn = _rms(pmc[:, MLA_Q_RANK:], gkv_ref[...]).astype(BF16)
    q2 = _dot(cqn, wq_ref[...])
    kv2 = _dot(ckvn, wkv_ref[...])
    pkr = _dot(xn, wkr_ref[...])
    kr = pkr[:, :LANE] * ck_ref[...] + pkr[:, LANE:] * sk_ref[...]
    hw = MLA_HEADS * MLA_HEAD_PAD
    cq = cq_ref[...]
    sq = sq_ref[...]
    for hd in range(MLA_HEADS):
        lo, hi = hd * MLA_HEAD_PAD, (hd + 1) * MLA_HEAD_PAD
        mq_ref[:, lo:hi] = (q2[:, lo:hi] * cq + q2[:, hw + lo:hw + hi] * sq).astype(BF16)
        mk_ref[:, lo:hi] = (kv2[:, lo:hi] + kr).astype(BF16)
    mv_ref[...] = kv2[:, hw:].astype(BF16)

    ps = _dot(xn, wswa_ref[...])
    cs = cs_ref[...]
    ss = ss_ref[...]
    for j in range(SWA_WIDTH // LANE):
        lo, hi = j * LANE, (j + 1) * LANE
        swq_ref[:, lo:hi] = (ps[:, lo:hi] * cs
                             + ps[:, SWA_WIDTH + lo:SWA_WIDTH + hi] * ss).astype(BF16)
    o = 2 * SWA_WIDTH
    swk_ref[...] = (ps[:, o:o + SWA_KV_WIDTH] * cs
                    + ps[:, o + SWA_KV_WIDTH:o + 2 * SWA_KV_WIDTH] * ss).astype(BF16)
    swv_ref[...] = ps[:, o + 2 * SWA_KV_WIDTH:].astype(BF16)


def _inproj(h, lw, tabs, seq):
    t, d = h.shape
    tm = TOK_TILE
    nt = t // tm
    spt = seq // tm

    def full(a):
        return pl.BlockSpec(a.shape, lambda i: (0,) * a.ndim)

    def tab(a):
        return pl.BlockSpec((tm, a.shape[1]), lambda i: (i % spt, 0))

    def tok(w):
        return pl.BlockSpec((tm, w), lambda i: (i, 0))

    weights = [lw['g_attn'], lw['g_q'], lw['g_kv'], lw['w_na'], lw['w_mc'], lw['w_kr'],
               lw['w_swa'], lw['w_q_up'], lw['w_kv_up']]
    tables = [tabs['cq'], tabs['sq'], tabs['ck'], tabs['sk'], tabs['cs'], tabs['ss']]
    widths = [NA_WIDTH, NA_WIDTH, NA_WIDTH, MLA_HEADS * MLA_HEAD_PAD, MLA_HEADS * MLA_HEAD_PAD,
              MLA_HEADS * MLA_HEAD_PAD, SWA_WIDTH, SWA_KV_WIDTH, SWA_KV_WIDTH]
    return pl.pallas_call(
        _inproj_kernel,
        grid=(nt,),
        in_specs=[tok(d)] + [full(w) for w in weights] + [tab(a) for a in tables],
        out_specs=[tok(w) for w in widths],
        out_shape=[jax.ShapeDtypeStruct((t, w), BF16) for w in widths],
        compiler_params=_cparams(("parallel",)),
        name="inproj",
    )(h, *weights, *tables)


def _na_kernel(q_ref, k_ref, v_ref, tbl_ref, o_ref, *, n_blocks):
    blk = pl.program_id(1)
    last_start = (n_blocks * NA_QROWS - NA_KROWS)
    ks = jnp.clip(blk * NA_QROWS - NA_WIN_ROWS // 2, 0, last_start) * GRID_W
    ks = pl.multiple_of(ks, GRID_W)
    nk = NA_KROWS * GRID_W
    outs = []
    for hd in range(NA_HEADS):
        lo, hi = hd * NA_HEAD_DIM, (hd + 1) * NA_HEAD_DIM
        q = q_ref[0, :, lo:hi]
        k = k_ref[0, pl.ds(ks, nk), lo:hi]
        v = v_ref[0, pl.ds(ks, nk), lo:hi]
        s = _dot_t(q, k) + tbl_ref[0, hd]
        m = jnp.max(s, axis=-1, keepdims=True)
        e = jnp.exp(s - m)
        l = jnp.sum(e, axis=-1, keepdims=True)
        outs.append(_dot(e.astype(BF16), v) / l)
    o_ref[0] = jnp.concatenate(outs, axis=-1).astype(BF16)


def _na_attention(q, k, v, tbl):
    b, s, w = q.shape
    rows = s // GRID_W
    nb = rows // NA_QROWS
    tq = NA_QROWS * GRID_W

    def tbl_map(bi, blk):
        sel = jnp.where(blk == 0, 0, jnp.where(blk == nb - 1, 2, 1))
        return (sel, 0, 0, 0)

    return pl.pallas_call(
        functools.partial(_na_kernel, n_blocks=nb),
        grid=(b, nb),
        in_specs=[pl.BlockSpec((1, tq, w), lambda bi, blk: (bi, blk, 0)),
                  pl.BlockSpec((1, s, w), lambda bi, blk: (bi, 0, 0)),
                  pl.BlockSpec((1, s, w), lambda bi, blk: (bi, 0, 0)),
                  pl.BlockSpec((1,) + tbl.shape[1:], tbl_map)],
        out_specs=pl.BlockSpec((1, tq, w), lambda bi, blk: (bi, blk, 0)),
        out_shape=jax.ShapeDtypeStruct((b, s, w), BF16),
        compiler_params=_cparams(("parallel", "arbitrary")),
        name="na_attn",
    )(q, k, v, tbl)


def _mla_kernel(q_ref, k_ref, v_ref, o_ref):
    for pair in range(MLA_HEADS_PER_STEP // 2):
        acc = None
        for hh in range(2 * pair, 2 * pair + 2):
            lo, hi = hh * MLA_HEAD_PAD, (hh + 1) * MLA_HEAD_PAD
            q = q_ref[0, :, lo:hi]
            k = k_ref[0, :, lo:hi]
            v = v_ref[0, :, lo:hi]
            s = _dot_t(q, k)
            m = jnp.max(s, axis=-1, keepdims=True)
            e = jnp.exp2(s - m)
            l = jnp.sum(e, axis=-1, keepdims=True)
            o = _dot(e.astype(BF16), v) / l
            acc = o if acc is None else acc + o
        o_ref[0, :, pair * LANE:(pair + 1) * LANE] = acc.astype(BF16)


def _mla_attention(q, k, v):
    b, s, _ = q.shape
    hps = MLA_HEADS_PER_STEP
    pw = hps * MLA_HEAD_PAD
    ow = hps * MLA_V_DIM
    ngrp = MLA_HEADS // hps
    nq = s // MLA_TQ
    return pl.pallas_call(
        _mla_kernel,
        grid=(b, ngrp, nq),
        in_specs=[pl.BlockSpec((1, MLA_TQ, pw), lambda bi, hg, qi: (bi, qi, hg)),
                  pl.BlockSpec((1, s, pw), lambda bi, hg, qi: (bi, 0, hg)),
                  pl.BlockSpec((1, s, pw), lambda bi, hg, qi: (bi, 0, hg))],
        out_specs=pl.BlockSpec((1, MLA_TQ, ow), lambda bi, hg, qi: (bi, qi, hg)),
        out_shape=jax.ShapeDtypeStruct((b, s, MLA_WIDTH), BF16),
        compiler_params=_cparams(("parallel", "parallel", "arbitrary")),
        name="mla_attn",
    )(q, k, v)


def _swa_kernel(sink_ref, q_ref, k_ref, v_ref, o_ref, *, seq):
    n = pl.program_id(1)
    blk = SWA_BLOCK
    nk = 3 * blk
    ks = pl.multiple_of(jnp.clip((n - 1) * blk, 0, seq - nk), blk)
    group = SWA_HEADS // SWA_KV_HEADS
    rows = group * blk
    ri = lax.broadcasted_iota(jnp.int32, (rows, nk), 0)
    ci = lax.broadcasted_iota(jnp.int32, (rows, nk), 1)
    rel = (ks + ci) - (n * blk + (ri & (blk - 1)))
    valid = jnp.abs(rel) <= SWA_WINDOW
    rcol = lax.broadcasted_iota(jnp.int32, (rows, 1), 0)
    outs = []
    for kvh in range(SWA_KV_HEADS):
        qs = []
        sink = jnp.zeros((rows, 1), F32)
        for g in range(group):
            hd = kvh * group + g
            qs.append(q_ref[0, :, hd * SWA_HEAD_DIM:(hd + 1) * SWA_HEAD_DIM])
            sink = jnp.where((rcol >= g * blk) & (rcol < (g + 1) * blk), sink_ref[hd], sink)
        q = jnp.concatenate(qs, axis=0)
        lo, hi = kvh * SWA_HEAD_DIM, (kvh + 1) * SWA_HEAD_DIM
        k = k_ref[0, pl.ds(ks, nk), lo:hi]
        v = v_ref[0, pl.ds(ks, nk), lo:hi]
        s = jnp.where(valid, _dot_t(q, k), NEG_INF)
        m = jnp.maximum(jnp.max(s, axis=-1, keepdims=True), sink)
        e = jnp.exp(s - m)
        den = jnp.sum(e, axis=-1, keepdims=True) + jnp.exp(sink - m)
        o = _dot(e.astype(BF16), v) / den
        for g in range(group):
            outs.append(o[g * blk:(g + 1) * blk])
    o_ref[0] = jnp.concatenate(outs, axis=-1).astype(BF16)


def _swa_attention(q, k, v, sink):
    b, s, w = q.shape
    kw = k.shape[2]
    nb = s // SWA_BLOCK
    return pl.pallas_call(
        functools.partial(_swa_kernel, seq=s),
        grid=(b, nb),
        in_specs=[pl.BlockSpec(memory_space=pltpu.SMEM),
                  pl.BlockSpec((1, SWA_BLOCK, w), lambda bi, n: (bi, n, 0)),
                  pl.BlockSpec((1, s, kw), lambda bi, n: (bi, 0, 0)),
                  pl.BlockSpec((1, s, kw), lambda bi, n: (bi, 0, 0))],
        out_specs=pl.BlockSpec((1, SWA_BLOCK, w), lambda bi, n: (bi, n, 0)),
        out_shape=jax.ShapeDtypeStruct((b, s, w), BF16),
        compiler_params=_cparams(("parallel", "arbitrary")),
        name="swa_attn",
    )(sink, q, k, v)


def _outproj_kernel(h_ref, ona_ref, omla_ref, oswa_ref, wna_ref, wmla_ref, wswa_ref,
                    g_ref, wr_ref, h1_ref, aff_ref):
    h1 = (h_ref[...] + _dot(ona_ref[...], wna_ref[...]) + _dot(omla_ref[...], wmla_ref[...])
          + _dot(oswa_ref[...], wswa_ref[...]))
    h1_ref[...] = h1
    hn = _rms(h1, g_ref[...])
    hn_hi = hn.astype(BF16)
    hn_lo = (hn - hn_hi.astype(F32)).astype(BF16)
    wr = wr_ref[...]
    p_hi = _dot_t(wr, hn_hi)
    p_lo = _dot_t(wr, hn_lo)
    logits = p_hi[:N_EXPERTS] + p_hi[N_EXPERTS:] + p_lo[:N_EXPERTS]
    m = jnp.max(logits, axis=0, keepdims=True)
    e = jnp.exp(logits - m)
    aff_ref[0] = e / jnp.sum(e, axis=0, keepdims=True)


def _outproj(h, o_na, o_mla, o_swa, lw, batch, seq):
    t, d = h.shape
    tm = TOK_TILE
    spt = seq // tm

    def full(a):
        return pl.BlockSpec(a.shape, lambda i: (0,) * a.ndim)

    def tok(w):
        return pl.BlockSpec((tm, w), lambda i: (i, 0))

    weights = [lw['w_out_na'], lw['w_out_mla'], lw['w_out_swa'], lw['g_ffn'], lw['w_router']]
    return pl.pallas_call(
        _outproj_kernel,
        grid=(t // tm,),
        in_specs=[tok(d), tok(NA_WIDTH), tok(MLA_WIDTH), tok(SWA_WIDTH)] + [full(w) for w in weights],
        out_specs=[tok(d), pl.BlockSpec((1, N_EXPERTS, tm), lambda i: (i // spt, 0, i % spt))],
        out_shape=[jax.ShapeDtypeStruct((t, d), F32),
                   jax.ShapeDtypeStruct((batch, N_EXPERTS, seq), F32)],
        compiler_params=_cparams(("parallel",)),
        name="outproj_router",
    )(h, o_na, o_mla, o_swa, *weights)


def _select_kernel(aff_ref, tri_ref, idx_ref, csum_scr, *, cap):
    x = aff_ref[0]
    ne, seq = x.shape
    capf = float(cap)

    def count_ge(v):
        return jnp.sum(jnp.where(x >= v, 1.0, 0.0), axis=1, keepdims=True)

    def search(i, tb):
        cand = tb | jnp.left_shift(jnp.int32(1), 30 - i)
        return jnp.where(count_ge(pltpu.bitcast(cand, F32)) >= capf, cand, tb)

    tb = lax.fori_loop(0, 31, search, jnp.zeros((ne, 1), jnp.int32))

    def refine(i, lohi):
        lo, hi = lohi
        mid = 0.5 * (lo + hi)
        ok = count_ge(mid) >= capf
        return jnp.where(ok, mid, lo), jnp.where(ok, hi, mid)

    thr, _ = lax.fori_loop(0, 24, refine, (pltpu.bitcast(tb, F32), pltpu.bitcast(tb + 1, F32)))
    gt = jnp.where(x > thr, 1.0, 0.0)
    eq = jnp.where(x == thr, 1.0, 0.0)
    need = capf - jnp.sum(gt, axis=1, keepdims=True)
    tri = tri_ref[...]

    def cumsum(mask):
        outs = []
        off = jnp.zeros((ne, 1), F32)
        for c in range(seq // LANE):
            cs = _dot(mask[:, c * LANE:(c + 1) * LANE].astype(BF16), tri) + off
            outs.append(cs)
            off = cs[:, LANE - 1:LANE]
        return jnp.concatenate(outs, axis=1)

    tie_rank = cumsum(eq) - eq
    sel = gt + eq * jnp.where(tie_rank < need, 1.0, 0.0)
    csum = cumsum(sel)
    for e in range(ne):
        csum_scr[e] = csum[e:e + 1, :]

    n_chunks = seq // LANE
    slot_blocks = cap // LANE

    def per_expert(e, carry):
        for sb in range(slot_blocks):
            slot = (lax.broadcasted_iota(jnp.int32, (LANE, LANE), 0) + sb * LANE).astype(F32)

            def chunk(c, acc):
                off = pl.multiple_of(c * LANE, LANE)
                return acc + jnp.where(csum_scr[e, :, pl.ds(off, LANE)] <= slot, 1.0, 0.0)

            acc = lax.fori_loop(0, n_chunks, chunk, jnp.zeros((LANE, LANE), F32), unroll=4)
            idx_ref[0, e, sb * LANE:(sb + 1) * LANE, :] = (
                jnp.sum(acc, axis=1, keepdims=True).astype(jnp.int32))
        return carry

    lax.fori_loop(0, ne, per_expert, 0)


def _select(aff, tri, cap):
    b, ne, seq = aff.shape
    return pl.pallas_call(
        functools.partial(_select_kernel, cap=cap),
        grid=(b,),
        in_specs=[pl.BlockSpec((1, ne, seq), lambda bi: (bi, 0, 0)),
                  pl.BlockSpec(tri.shape, lambda bi: (0, 0))],
        out_specs=pl.BlockSpec((1, ne, cap, 1), lambda bi: (bi, 0, 0, 0)),
        out_shape=jax.ShapeDtypeStruct((b, ne, cap, 1), jnp.int32),
        scratch_shapes=[pltpu.VMEM((ne, 1, seq), F32)],
        compiler_params=_cparams(("parallel",)),
        name="ec_select",
    )(aff, tri)


GATHER_ROWS = 8
SCATTER_ROWS = 8


def _moe_kernel(idx_ref, aff_ref, h_ref, g_ref, wg_ref, wu_ref, wd_ref, o_ref, xs_scr, y_scr, *, cap):
    @pl.when(pl.program_id(1) == 0)
    def _():
        o_ref[...] = jnp.zeros_like(o_ref)

    for j in range(cap // GATHER_ROWS):
        base = j * GATHER_ROWS
        rows = [h_ref[0, pl.ds(idx_ref[0, 0, base + r], 1), :] for r in range(GATHER_ROWS)]
        xs_scr[base:base + GATHER_ROWS, :] = jnp.concatenate(rows, axis=0)

    xn = _rms(xs_scr[...], g_ref[...]).astype(BF16)
    a = _dot(xn, wg_ref[0])
    u = _dot(xn, wu_ref[0])
    mid = (a * jax.nn.sigmoid(a) * u).astype(BF16)
    y_scr[...] = _dot(mid, wd_ref[0])

    for j in range(cap // SCATTER_ROWS):
        base = j * SCATTER_ROWS
        toks = [idx_ref[0, 0, base + r] for r in range(SCATTER_ROWS)]
        rows = [o_ref[0, pl.ds(toks[r], 1), :] for r in range(SCATTER_ROWS)]
        for r in range(SCATTER_ROWS):
            gate = aff_ref[0, 0, toks[r]]
            o_ref[0, pl.ds(toks[r], 1), :] = rows[r] + gate * y_scr[base + r:base + r + 1, :]


def _moe(idx, aff, h1, lw, cap):
    b, seq, d = h1.shape
    ne = N_EXPERTS
    ff = EXPERT_FF
    idx3 = idx.reshape(b * ne, 1, cap)
    aff3 = aff.reshape(b * ne, 1, seq)
    once = pl.Buffered(1)
    return pl.pallas_call(
        functools.partial(_moe_kernel, cap=cap),
        grid=(b, ne),
        in_specs=[pl.BlockSpec((1, 1, cap), lambda bi, e: (bi * ne + e, 0, 0),
                               memory_space=pltpu.SMEM),
                  pl.BlockSpec((1, 1, seq), lambda bi, e: (bi * ne + e, 0, 0),
                               memory_space=pltpu.SMEM),
                  pl.BlockSpec((1, seq, d), lambda bi, e: (bi, 0, 0), pipeline_mode=once),
                  pl.BlockSpec((1, d), lambda bi, e: (0, 0)),
                  pl.BlockSpec((1, d, ff), lambda bi, e: (e, 0, 0)),
                  pl.BlockSpec((1, d, ff), lambda bi, e: (e, 0, 0)),
                  pl.BlockSpec((1, ff, d), lambda bi, e: (e, 0, 0))],
        out_specs=pl.BlockSpec((1, seq, d), lambda bi, e: (bi, 0, 0), pipeline_mode=once),
        out_shape=jax.ShapeDtypeStruct((b, seq, d), F32),
        scratch_shapes=[pltpu.VMEM((cap, d), F32), pltpu.VMEM((cap, d), F32)],
        compiler_params=_cparams(("parallel", "arbitrary")),
        name="ec_moe",
    )(idx3, aff3, h1, lw['g_ffn'], lw['w_gate'], lw['w_up'], lw['w_down'])


def _ple_kernel(h_ref, moe_ref, p_ref, g_ref, wg_ref, wp_ref, gf_ref, o_ref, *, final):
    h = h_ref[...] + moe_ref[...]
    gate = jax.nn.sigmoid(_dot(_rms(h, g_ref[...]).astype(BF16), wg_ref[...]))
    out = h + gate * _dot(p_ref[...].astype(BF16), wp_ref[...])
    if final:
        out = _rms(out, gf_ref[...])
    o_ref[...] = out


def _ple(h, moe, p, lw, g_final, final):
    t, d = h.shape
    tm = TOK_TILE

    def full(a):
        return pl.BlockSpec(a.shape, lambda i: (0,) * a.ndim)

    def tok(w):
        return pl.BlockSpec((tm, w), lambda i: (i, 0))

    weights = [lw['g_ple'], lw['w_ple_gate'], lw['w_ple_proj'], g_final]
    return pl.pallas_call(
        functools.partial(_ple_kernel, final=final),
        grid=(t // tm,),
        in_specs=[tok(d), tok(d), tok(p.shape[1])] + [full(w) for w in weights],
        out_specs=tok(d),
        out_shape=jax.ShapeDtypeStruct((t, d), F32),
        compiler_params=_cparams(("parallel",)),
        name="ple",
    )(h, moe, p, *weights)


def _rot_cols(w, d):
    g = w.reshape(w.shape[:-1] + (w.shape[-1] // d, d))
    return jnp.concatenate([-g[..., d // 2:], g[..., :d // 2]], axis=-1).reshape(w.shape)


def _rope_tables(seq):
    pos = jnp.arange(seq, dtype=F32)

    def cs(dim):
        inv = 1.0 / (ROPE_THETA ** (jnp.arange(0, dim, 2, dtype=F32) / dim))
        ang = pos[:, None] * inv[None, :]
        return jnp.cos(ang), jnp.sin(ang)

    cm, sm = cs(MLA_ROPE_DIM)
    cw, sw = cs(SWA_HEAD_DIM)
    z = lambda n: jnp.zeros((seq, n), F32)
    one = jnp.ones((seq, MLA_NOPE_DIM), F32)
    pad = MLA_HEAD_PAD - MLA_NOPE_DIM - MLA_ROPE_DIM
    qs = (MLA_NOPE_DIM + MLA_ROPE_DIM) ** -0.5 * math.log2(math.e)
    return {
        'cq': qs * jnp.concatenate([one, cm, cm, z(pad)], axis=1),
        'sq': qs * jnp.concatenate([z(MLA_NOPE_DIM), sm, sm, z(pad)], axis=1),
        'ck': jnp.concatenate([z(MLA_NOPE_DIM), cm, cm, z(pad)], axis=1),
        'sk': jnp.concatenate([z(MLA_NOPE_DIM), sm, sm, z(pad)], axis=1),
        'cs': jnp.concatenate([cw, cw] * (LANE // SWA_HEAD_DIM), axis=1),
        'ss': jnp.concatenate([sw, sw] * (LANE // SWA_HEAD_DIM), axis=1),
    }


def _na_bias_tables(rpb):
    nh = rpb.shape[0]
    qi = jnp.arange(NA_QROWS)[:, None]
    ki = jnp.arange(NA_KROWS)[None, :]
    half = NA_WIN_ROWS // 2
    cases = [
        ((ki < NA_WIN_ROWS) & (qi >= 0), ki - qi),
        ((ki >= qi) & (ki < qi + NA_WIN_ROWS), ki - half - qi),
        ((ki >= NA_KROWS - NA_WIN_ROWS) & (qi >= 0), ki - (NA_KROWS - NA_QROWS) - qi),
    ]
    row_ok = jnp.stack([ok for ok, _ in cases])
    dri = jnp.stack([jnp.clip(dr + NA_WIN_ROWS - 1, 0, 2 * NA_WIN_ROWS - 2) for _, dr in cases])
    c = jnp.arange(GRID_W)[:, None]
    kc = jnp.arange(GRID_W)[None, :]
    c0 = jnp.clip(c - NA_WIN_COLS // 2, 0, GRID_W - NA_WIN_COLS)
    col_ok = (kc >= c0) & (kc < c0 + NA_WIN_COLS)
    dci = jnp.clip(kc - c, -(NA_WIN_COLS - 1), NA_WIN_COLS - 1) + NA_WIN_COLS - 1
    one_r = jax.nn.one_hot(dri, 2 * NA_WIN_ROWS - 1, dtype=F32)
    one_c = jax.nn.one_hot(dci, 2 * NA_WIN_COLS - 1, dtype=F32)
    bias = jnp.einsum('xqka,hab,cnb->xhqckn', one_r, rpb.astype(F32), one_c,
                      precision=lax.Precision.HIGHEST)
    ok = row_ok[:, None, :, None, :, None] & col_ok[None, None, None, :, None, :]
    tab = jnp.where(ok, bias, NEG_INF)
    return tab.reshape(3, nh, NA_QROWS * GRID_W, NA_KROWS * GRID_W)


def _layer_weights(i, attn_norm, w_in, mla_q_norm, mla_w_q_up, mla_kv_norm, mla_w_kv_up, w_out,
                   ffn_norm, w_router, w_gate_e, w_up_e, w_down_e, ple_norm, w_ple_gate,
                   w_ple_proj):
    offs = [0]
    for n in IN_SIZES:
        offs.append(offs[-1] + n)
    parts = [w_in[i][:, offs[j]:offs[j + 1]] for j in range(len(IN_SIZES))]
    na_q, na_k, na_v, m_cq, m_ckv, m_kr, s_q, s_k, s_v = parts
    d = w_in.shape[1]
    na_scale = NA_HEAD_DIM ** -0.5
    swa_scale = SWA_HEAD_DIM ** -0.5

    zk = jnp.zeros((d, MLA_NOPE_DIM), F32)
    zp = jnp.zeros((d, MLA_HEAD_PAD - MLA_NOPE_DIM - MLA_ROPE_DIM), F32)
    w_kr = jnp.concatenate([zk, m_kr, zp, zk, _rot_cols(m_kr, MLA_ROPE_DIM), zp], axis=1)

    qu = mla_w_q_up[i].reshape(MLA_Q_RANK, MLA_HEADS, MLA_NOPE_DIM + MLA_ROPE_DIM)
    q_nope, q_rope = qu[..., :MLA_NOPE_DIM], qu[..., MLA_NOPE_DIM:]
    zq = jnp.zeros_like(q_rope)
    q_main = jnp.concatenate([q_nope, q_rope, zq], axis=-1)
    q_rot = jnp.concatenate([jnp.zeros_like(q_nope), _rot_cols(q_rope, MLA_ROPE_DIM), zq], axis=-1)
    w_q_up = jnp.concatenate([q_main.reshape(MLA_Q_RANK, -1), q_rot.reshape(MLA_Q_RANK, -1)], axis=1)

    kvu = mla_w_kv_up[i].reshape(MLA_KV_RANK, MLA_HEADS, MLA_NOPE_DIM + MLA_V_DIM)
    k_nope, v_up = kvu[..., :MLA_NOPE_DIM], kvu[..., MLA_NOPE_DIM:]
    zv = jnp.zeros_like(v_up)
    k_main = jnp.concatenate([k_nope, jnp.zeros_like(k_nope)], axis=-1)
    even = (jnp.arange(MLA_HEADS) % 2 == 0)[None, :, None]
    v_main = jnp.concatenate([jnp.where(even, v_up, zv), jnp.where(even, zv, v_up)], axis=-1)
    w_kv_up = jnp.concatenate([k_main.reshape(MLA_KV_RANK, -1), v_main.reshape(MLA_KV_RANK, -1)],
                              axis=1)

    wr = w_router[i].T
    wr_hi = wr.astype(BF16)
    wr_lo = (wr - wr_hi.astype(F32)).astype(BF16)
    wo = w_out[i]
    row = lambda g: g.reshape(1, -1).astype(F32)
    return {
        'g_attn': row(attn_norm[i]), 'g_q': row(mla_q_norm[i]), 'g_kv': row(mla_kv_norm[i]),
        'w_na': jnp.concatenate([na_q * na_scale, na_k, na_v], axis=1).astype(BF16),
        'w_mc': jnp.concatenate([m_cq, m_ckv], axis=1).astype(BF16),
        'w_kr': w_kr.astype(BF16),
        'w_swa': jnp.concatenate([s_q * swa_scale, _rot_cols(s_q, SWA_HEAD_DIM) * swa_scale,
                                  s_k, _rot_cols(s_k, SWA_HEAD_DIM), s_v], axis=1).astype(BF16),
        'w_q_up': w_q_up.astype(BF16), 'w_kv_up': w_kv_up.astype(BF16),
        'w_out_na': wo[:NA_WIDTH].astype(BF16),
        'w_out_mla': wo[NA_WIDTH:NA_WIDTH + MLA_WIDTH].astype(BF16),
        'w_out_swa': wo[NA_WIDTH + MLA_WIDTH:].astype(BF16),
        'g_ffn': row(ffn_norm[i]),
        'w_router': jnp.concatenate([wr_hi, wr_lo], axis=0),
        'w_gate': w_gate_e[i].astype(BF16), 'w_up': w_up_e[i].astype(BF16),
        'w_down': w_down_e[i].astype(BF16),
        'g_ple': row(ple_norm[i]),
        'w_ple_gate': w_ple_gate[i].astype(BF16), 'w_ple_proj': w_ple_proj[i].astype(BF16),
    }


def kernel(x, p, attn_norm, w_in, na_rpb, mla_q_norm, mla_w_q_up, mla_kv_norm, mla_w_kv_up,
           swa_sink, w_out, ffn_norm, w_router, w_gate_e, w_up_e, w_down_e, ple_norm, w_ple_gate,
           w_ple_proj, final_norm):
    b, seq, d = x.shape
    depth = w_in.shape[0]
    t = b * seq
    cap = EC_CAPACITY_FACTOR * seq // N_EXPERTS
    tabs = _rope_tables(seq)
    tri = jnp.triu(jnp.ones((LANE, LANE), F32)).astype(BF16)
    g_final = final_norm.reshape(1, -1).astype(F32)

    h = x.reshape(t, d)
    for i in range(depth):
        lw = _layer_weights(i, attn_norm, w_in, mla_q_norm, mla_w_q_up, mla_kv_norm, mla_w_kv_up,
                            w_out, ffn_norm, w_router, w_gate_e, w_up_e, w_down_e, ple_norm,
                            w_ple_gate, w_ple_proj)
        (na_q, na_k, na_v, m_q, m_k, m_v, s_q, s_k, s_v) = _inproj(h, lw, tabs, seq)
        r3 = lambda a: a.reshape(b, seq, a.shape[-1])
        o_na = _na_attention(r3(na_q), r3(na_k), r3(na_v), _na_bias_tables(na_rpb[i]))
        o_mla = _mla_attention(r3(m_q), r3(m_k), r3(m_v))
        o_swa = _swa_attention(r3(s_q), r3(s_k), r3(s_v), swa_sink[i].astype(F32))
        r2 = lambda a: a.reshape(t, a.shape[-1])
        h1, aff = _outproj(h, r2(o_na), r2(o_mla), r2(o_swa), lw, b, seq)
        idx = _select(aff, tri, cap).reshape(b, N_EXPERTS, cap)
        moe = _moe(idx, aff, h1.reshape(b, seq, d), lw, cap)
        h = _ple(h1, moe.reshape(t, d), p[i].reshape(t, PLE_DIM), lw, g_final, i == depth - 1)
    return h.reshape(b, seq, d)
```

```python
import functools
import math

import jax
import jax.numpy as jnp
from jax import lax
from jax.experimental import pallas as pl
from jax.experimental.pallas import tpu as pltpu

F32 = jnp.float32
BF16 = jnp.bfloat16

GRID_W = 64
NA_HEADS = 4
NA_HEAD_DIM = 64
NA_WIN_ROWS = 8
NA_WIN_COLS = 16
MLA_HEADS = 8
MLA_Q_RANK = 384
MLA_KV_RANK = 256
MLA_NOPE_DIM = 64
MLA_ROPE_DIM = 32
MLA_V_DIM = 64
SWA_HEADS = 4
SWA_KV_HEADS = 2
SWA_HEAD_DIM = 64
SWA_WINDOW = 128
SWA_BLOCK = 128
N_EXPERTS = 16
EXPERT_FF = 512
EC_CAPACITY_FACTOR = 2
PLE_DIM = 256
ROPE_THETA = 10000.0
NORM_EPS = 1e-6
NEG_INF = -1e30

NA_WIDTH = NA_HEADS * NA_HEAD_DIM
MLA_WIDTH = MLA_HEADS * MLA_V_DIM
SWA_WIDTH = SWA_HEADS * SWA_HEAD_DIM
SWA_KV_WIDTH = SWA_KV_HEADS * SWA_HEAD_DIM
IN_SIZES = (NA_WIDTH, NA_WIDTH, NA_WIDTH, MLA_Q_RANK, MLA_KV_RANK, MLA_ROPE_DIM,
            SWA_WIDTH, SWA_KV_WIDTH, SWA_KV_WIDTH)

LANE = 128
SUBLANES = 8
MLA_HEAD_PAD = LANE
VMEM_LIMIT = 56 * 1024 * 1024

TOK_TILE = 512
NA_QROWS = 4
NA_KROWS = 12
NA_QBLOCKS = 4
SWA_QBLOCKS = 4
MLA_TQ = 256
MLA_HEADS_PER_STEP = 8


def _cparams(sem):
    return pltpu.CompilerParams(dimension_semantics=sem, vmem_limit_bytes=VMEM_LIMIT)


def _dot(a, b):
    return jnp.dot(a, b, preferred_element_type=F32)


def _dot_t(a, b):
    return lax.dot_general(a, b, (((1,), (1,)), ((), ())), preferred_element_type=F32)


def _rms(x, g):
    return x * lax.rsqrt(jnp.mean(x * x, axis=-1, keepdims=True) + NORM_EPS) * g


def _inproj_kernel(h_ref, g_ref, gq_ref, gkv_ref, wna_ref, wmc_ref, wkr_ref, wswa_ref,
                   wq_ref, wkv_ref, cq_ref, sq_ref, ck_ref, sk_ref, cs_ref, ss_ref,
                   naq_ref, nak_ref, nav_ref, mq_ref, mk_ref, mv_ref,
                   swq_ref, swk_ref, swv_ref):
    xn = _rms(h_ref[...], g_ref[...]).astype(BF16)

    pna = _dot(xn, wna_ref[...])
    naq_ref[...] = pna[:, :NA_WIDTH].astype(BF16)
    nak_ref[...] = pna[:, NA_WIDTH:2 * NA_WIDTH].astype(BF16)
    nav_ref[...] = pna[:, 2 * NA_WIDTH:].astype(BF16)

    pmc = _dot(xn, wmc_ref[...])
    cqn = _rms(pmc[:, :MLA_Q_RANK], gq_ref[...]).astype(BF16)
    ckvn = _rms(pmc[:, MLA_Q_RANK:], gkv_ref[...]).astype(BF16)
    q2 = _dot(cqn, wq_ref[...])
    kv2 = _dot(ckvn, wkv_ref[...])
    pkr = _dot(xn, wkr_ref[...])
    kr = pkr[:, :LANE] * ck_ref[...] + pkr[:, LANE:] * sk_ref[...]
    hw = MLA_HEADS * MLA_HEAD_PAD
    cq = cq_ref[...]
    sq = sq_ref[...]
    for hd in range(MLA_HEADS):
        lo, hi = hd * MLA_HEAD_PAD, (hd + 1) * MLA_HEAD_PAD
        mq_ref[:, lo:hi] = (q2[:, lo:hi] * cq + q2[:, hw + lo:hw + hi] * sq).astype(BF16)
        mk_ref[:, lo:hi] = (kv2[:, lo:hi] + kr).astype(BF16)
    mv_ref[...] = kv2[:, hw:].astype(BF16)

    ps = _dot(xn, wswa_ref[...])
    cs = cs_ref[...]
    ss = ss_ref[...]
    for j in range(SWA_WIDTH // LANE):
        lo, hi = j * LANE, (j + 1) * LANE
        swq_ref[:, lo:hi] = (ps[:, lo:hi] * cs
                             + ps[:, SWA_WIDTH + lo:SWA_WIDTH + hi] * ss).astype(BF16)
    o = 2 * SWA_WIDTH
    swk_ref[...] = (ps[:, o:o + SWA_KV_WIDTH] * cs
                    + ps[:, o + SWA_KV_WIDTH:o + 2 * SWA_KV_WIDTH] * ss).astype(BF16)
    swv_ref[...] = ps[:, o + 2 * SWA_KV_WIDTH:].astype(BF16)


def _inproj(h, lw, tabs, seq):
    t, d = h.shape
    tm = TOK_TILE
    nt = t // tm
    spt = seq // tm

    def full(a):
        return pl.BlockSpec(a.shape, lambda i: (0,) * a.ndim)

    def tab(a):
        return pl.BlockSpec((tm, a.shape[1]), lambda i: (i % spt, 0))

    def tok(w):
        return pl.BlockSpec((tm, w), lambda i: (i, 0))

    weights = [lw['g_attn'], lw['g_q'], lw['g_kv'], lw['w_na'], lw['w_mc'], lw['w_kr'],
               lw['w_swa'], lw['w_q_up'], lw['w_kv_up']]
    tables = [tabs['cq'], tabs['sq'], tabs['ck'], tabs['sk'], tabs['cs'], tabs['ss']]
    widths = [NA_WIDTH, NA_WIDTH, NA_WIDTH, MLA_HEADS * MLA_HEAD_PAD, MLA_HEADS * MLA_HEAD_PAD,
              MLA_HEADS * MLA_HEAD_PAD, SWA_WIDTH, SWA_KV_WIDTH, SWA_KV_WIDTH]
    return pl.pallas_call(
        _inproj_kernel,
        grid=(nt,),
        in_specs=[tok(d)] + [full(w) for w in weights] + [tab(a) for a in tables],
        out_specs=[tok(w) for w in widths],
        out_shape=[jax.ShapeDtypeStruct((t, w), BF16) for w in widths],
        compiler_params=_cparams(("parallel",)),
        name="inproj",
    )(h, *weights, *tables)


def _na_kernel(q_ref, k_ref, v_ref, tbl_ref, o_ref, *, n_blocks):
    last_start = (n_blocks * NA_QROWS - NA_KROWS)
    nk = NA_KROWS * GRID_W
    tq = NA_QROWS * GRID_W
    for sub in range(NA_QBLOCKS):
        blk = pl.program_id(1) * NA_QBLOCKS + sub
        ks = jnp.clip(blk * NA_QROWS - NA_WIN_ROWS // 2, 0, last_start) * GRID_W
        ks = pl.multiple_of(ks, GRID_W)
        sel = jnp.where(blk == 0, 0, jnp.where(blk == n_blocks - 1, 2, 1))
        outs = []
        for hd in range(NA_HEADS):
            lo, hi = hd * NA_HEAD_DIM, (hd + 1) * NA_HEAD_DIM
            q = q_ref[0, sub * tq:(sub + 1) * tq, lo:hi]
            k = k_ref[0, pl.ds(ks, nk), lo:hi]
            v = v_ref[0, pl.ds(ks, nk), lo:hi]
            s = _dot_t(q, k) + tbl_ref[sel, hd]
            m = jnp.max(s, axis=-1, keepdims=True)
            e = jnp.exp(s - m)
            l = jnp.sum(e, axis=-1, keepdims=True)
            outs.append(_dot(e.astype(BF16), v) / l)
        o_ref[0, sub * tq:(sub + 1) * tq, :] = jnp.concatenate(outs, axis=-1).astype(BF16)


def _na_attention(q, k, v, tbl):
    b, s, w = q.shape
    rows = s // GRID_W
    nb = rows // NA_QROWS
    tq = NA_QROWS * GRID_W * NA_QBLOCKS
    return pl.pallas_call(
        functools.partial(_na_kernel, n_blocks=nb),
        grid=(b, nb // NA_QBLOCKS),
        in_specs=[pl.BlockSpec((1, tq, w), lambda bi, blk: (bi, blk, 0)),
                  pl.BlockSpec((1, s, w), lambda bi, blk: (bi, 0, 0)),
                  pl.BlockSpec((1, s, w), lambda bi, blk: (bi, 0, 0)),
                  pl.BlockSpec(tbl.shape, lambda bi, blk: (0, 0, 0, 0))],
        out_specs=pl.BlockSpec((1, tq, w), lambda bi, blk: (bi, blk, 0)),
        out_shape=jax.ShapeDtypeStruct((b, s, w), BF16),
        compiler_params=_cparams(("parallel", "arbitrary")),
        name="na_attn",
    )(q, k, v, tbl)


def _mla_kernel(q_ref, k_ref, v_ref, o_ref):
    for pair in range(MLA_HEADS_PER_STEP // 2):
        acc = None
        for hh in range(2 * pair, 2 * pair + 2):
            lo, hi = hh * MLA_HEAD_PAD, (hh + 1) * MLA_HEAD_PAD
            q = q_ref[0, :, lo:hi]
            k = k_ref[0, :, lo:hi]
            v = v_ref[0, :, lo:hi]
            s = _dot_t(q, k)
            m = jnp.max(s, axis=-1, keepdims=True)
            e = jnp.exp2(s - m)
            l = jnp.sum(e, axis=-1, keepdims=True)
            o = _dot(e.astype(BF16), v) / l
            acc = o if acc is None else acc + o
        o_ref[0, :, pair * LANE:(pair + 1) * LANE] = acc.astype(BF16)


def _mla_attention(q, k, v):
    b, s, _ = q.shape
    hps = MLA_HEADS_PER_STEP
    pw = hps * MLA_HEAD_PAD
    ow = hps * MLA_V_DIM
    ngrp = MLA_HEADS // hps
    nq = s // MLA_TQ
    return pl.pallas_call(
        _mla_kernel,
        grid=(b, ngrp, nq),
        in_specs=[pl.BlockSpec((1, MLA_TQ, pw), lambda bi, hg, qi: (bi, qi, hg)),
                  pl.BlockSpec((1, s, pw), lambda bi, hg, qi: (bi, 0, hg)),
                  pl.BlockSpec((1, s, pw), lambda bi, hg, qi: (bi, 0, hg))],
        out_specs=pl.BlockSpec((1, MLA_TQ, ow), lambda bi, hg, qi: (bi, qi, hg)),
        out_shape=jax.ShapeDtypeStruct((b, s, MLA_WIDTH), BF16),
        compiler_params=_cparams(("parallel", "parallel", "arbitrary")),
        name="mla_attn",
    )(q, k, v)


def _swa_kernel(sink_ref, q_ref, k_ref, v_ref, o_ref, *, seq):
    blk = SWA_BLOCK
    nk = 3 * blk
    group = SWA_HEADS // SWA_KV_HEADS
    rows = group * blk
    ri = lax.broadcasted_iota(jnp.int32, (rows, nk), 0)
    ci = lax.broadcasted_iota(jnp.int32, (rows, nk), 1)
    rcol = lax.broadcasted_iota(jnp.int32, (rows, 1), 0)
    for sub in range(SWA_QBLOCKS):
        n = pl.program_id(1) * SWA_QBLOCKS + sub
        ks = pl.multiple_of(jnp.clip((n - 1) * blk, 0, seq - nk), blk)
        rel = (ks + ci) - (n * blk + (ri & (blk - 1)))
        valid = jnp.abs(rel) <= SWA_WINDOW
        outs = []
        for kvh in range(SWA_KV_HEADS):
            qs = []
            sink = jnp.zeros((rows, 1), F32)
            for g in range(group):
                hd = kvh * group + g
                qs.append(q_ref[0, sub * blk:(sub + 1) * blk,
                                hd * SWA_HEAD_DIM:(hd + 1) * SWA_HEAD_DIM])
                sink = jnp.where((rcol >= g * blk) & (rcol < (g + 1) * blk), sink_ref[hd], sink)
            q = jnp.concatenate(qs, axis=0)
            lo, hi = kvh * SWA_HEAD_DIM, (kvh + 1) * SWA_HEAD_DIM
            k = k_ref[0, pl.ds(ks, nk), lo:hi]
            v = v_ref[0, pl.ds(ks, nk), lo:hi]
            s = jnp.where(valid, _dot_t(q, k), NEG_INF)
            m = jnp.maximum(jnp.max(s, axis=-1, keepdims=True), sink)
            e = jnp.exp(s - m)
            den = jnp.sum(e, axis=-1, keepdims=True) + jnp.exp(sink - m)
            o = _dot(e.astype(BF16), v) / den
            for g in range(group):
                outs.append(o[g * blk:(g + 1) * blk])
        o_ref[0, sub * blk:(sub + 1) * blk, :] = jnp.concatenate(outs, axis=-1).astype(BF16)


def _swa_attention(q, k, v, sink):
    b, s, w = q.shape
    kw = k.shape[2]
    tq = SWA_BLOCK * SWA_QBLOCKS
    return pl.pallas_call(
        functools.partial(_swa_kernel, seq=s),
        grid=(b, s // tq),
        in_specs=[pl.BlockSpec(memory_space=pltpu.SMEM),
                  pl.BlockSpec((1, tq, w), lambda bi, n: (bi, n, 0)),
                  pl.BlockSpec((1, s, kw), lambda bi, n: (bi, 0, 0)),
                  pl.BlockSpec((1, s, kw), lambda bi, n: (bi, 0, 0))],
        out_specs=pl.BlockSpec((1, tq, w), lambda bi, n: (bi, n, 0)),
        out_shape=jax.ShapeDtypeStruct((b, s, w), BF16),
        compiler_params=_cparams(("parallel", "arbitrary")),
        name="swa_attn",
    )(sink, q, k, v)


def _outproj_kernel(h_ref, ona_ref, omla_ref, oswa_ref, wna_ref, wmla_ref, wswa_ref,
                    g_ref, wr_ref, h1_ref, aff_ref):
    h1 = (h_ref[...] + _dot(ona_ref[...], wna_ref[...]) + _dot(omla_ref[...], wmla_ref[...])
          + _dot(oswa_ref[...], wswa_ref[...]))
    h1_ref[...] = h1
    hn = _rms(h1, g_ref[...])
    hn_hi = hn.astype(BF16)
    hn_lo = (hn - hn_hi.astype(F32)).astype(BF16)
    wr = wr_ref[...]
    p_hi = _dot_t(wr, hn_hi)
    p_lo = _dot_t(wr, hn_lo)
    logits = p_hi[:N_EXPERTS] + p_hi[N_EXPERTS:] + p_lo[:N_EXPERTS]
    m = jnp.max(logits, axis=0, keepdims=True)
    e = jnp.exp(logits - m)
    aff_ref[0] = e / jnp.sum(e, axis=0, keepdims=True)


def _outproj(h, o_na, o_mla, o_swa, lw, batch, seq):
    t, d = h.shape
    tm = TOK_TILE
    spt = seq // tm

    def full(a):
        return pl.BlockSpec(a.shape, lambda i: (0,) * a.ndim)

    def tok(w):
        return pl.BlockSpec((tm, w), lambda i: (i, 0))

    weights = [lw['w_out_na'], lw['w_out_mla'], lw['w_out_swa'], lw['g_ffn'], lw['w_router']]
    return pl.pallas_call(
        _outproj_kernel,
        grid=(t // tm,),
        in_specs=[tok(d), tok(NA_WIDTH), tok(MLA_WIDTH), tok(SWA_WIDTH)] + [full(w) for w in weights],
        out_specs=[tok(d), pl.BlockSpec((1, N_EXPERTS, tm), lambda i: (i // spt, 0, i % spt))],
        out_shape=[jax.ShapeDtypeStruct((t, d), F32),
                   jax.ShapeDtypeStruct((batch, N_EXPERTS, seq), F32)],
        compiler_params=_cparams(("parallel",)),
        name="outproj_router",
    )(h, o_na, o_mla, o_swa, *weights)


def _select_kernel(aff_ref, tri_ref, idx_ref, csum_scr, *, cap):
    x = aff_ref[0]
    ne, seq = x.shape
    capf = float(cap)

    def count_ge(v):
        return jnp.sum(jnp.where(x >= v, 1.0, 0.0), axis=1, keepdims=True)

    def search(i, tb):
        cand = tb | jnp.left_shift(jnp.int32(1), 30 - i)
        return jnp.where(count_ge(pltpu.bitcast(cand, F32)) >= capf, cand, tb)

    tb = lax.fori_loop(0, 31, search, jnp.zeros((ne, 1), jnp.int32))

    def refine(i, lohi):
        lo, hi = lohi
        mid = 0.5 * (lo + hi)
        ok = count_ge(mid) >= capf
        return jnp.where(ok, mid, lo), jnp.where(ok, hi, mid)

    thr, _ = lax.fori_loop(0, 24, refine, (pltpu.bitcast(tb, F32), pltpu.bitcast(tb + 1, F32)))
    gt = jnp.where(x > thr, 1.0, 0.0)
    eq = jnp.where(x == thr, 1.0, 0.0)
    need = capf - jnp.sum(gt, axis=1, keepdims=True)
    tri = tri_ref[...]

    def cumsum(mask):
        outs = []
        off = jnp.zeros((ne, 1), F32)
        for c in range(seq // LANE):
            cs = _dot(mask[:, c * LANE:(c + 1) * LANE].astype(BF16), tri) + off
            outs.append(cs)
            off = cs[:, LANE - 1:LANE]
        return jnp.concatenate(outs, axis=1)

    tie_rank = cumsum(eq) - eq
    sel = gt + eq * jnp.where(tie_rank < need, 1.0, 0.0)
    csum = cumsum(sel)
    for e in range(ne):
        csum_scr[e] = csum[e:e + 1, :]

    n_chunks = seq // LANE
    slot_blocks = cap // LANE

    def per_expert(e, carry):
        for sb in range(slot_blocks):
            slot = (lax.broadcasted_iota(jnp.int32, (LANE, LANE), 0) + sb * LANE).astype(F32)

            def chunk(c, acc):
                off = pl.multiple_of(c * LANE, LANE)
                return acc + jnp.where(csum_scr[e, :, pl.ds(off, LANE)] <= slot, 1.0, 0.0)

            acc = lax.fori_loop(0, n_chunks, chunk, jnp.zeros((LANE, LANE), F32), unroll=4)
            idx_ref[0, e, sb * LANE:(sb + 1) * LANE, :] = (
                jnp.sum(acc, axis=1, keepdims=True).astype(jnp.int32))
        return carry

    lax.fori_loop(0, ne, per_expert, 0)


def _select(aff, tri, cap):
    b, ne, seq = aff.shape
    return pl.pallas_call(
        functools.partial(_select_kernel, cap=cap),
        grid=(b,),
        in_specs=[pl.BlockSpec((1, ne, seq), lambda bi: (bi, 0, 0)),
                  pl.BlockSpec(tri.shape, lambda bi: (0, 0))],
        out_specs=pl.BlockSpec((1, ne, cap, 1), lambda bi: (bi, 0, 0, 0)),
        out_shape=jax.ShapeDtypeStruct((b, ne, cap, 1), jnp.int32),
        scratch_shapes=[pltpu.VMEM((ne, 1, seq), F32)],
        compiler_params=_cparams(("parallel",)),
        name="ec_select",
    )(aff, tri)


SCATTER_ROWS = 8


def _moe_kernel(idx_ref, aff_ref, h_ref, g_ref, wg_ref, wu_ref, wd_ref, o_ref, xs_scr, y_scr, *, cap):
    nch = SUBLANES
    @pl.when(pl.program_id(1) == 0)
    def _():
        o_ref[...] = jnp.zeros_like(o_ref)

    def tile_of(tok):
        return pl.ds(pl.multiple_of(tok * nch, nch), nch)

    for i in range(cap):
        xs_scr[i * nch:(i + 1) * nch, :] = h_ref[0, tile_of(idx_ref[0, 0, i]), :]

    xs = jnp.concatenate([xs_scr[pl.ds(j, cap, stride=nch), :] for j in range(nch)], axis=1)
    xn = _rms(xs, g_ref[...]).astype(BF16)
    a = _dot(xn, wg_ref[0].astype(BF16))
    u = _dot(xn, wu_ref[0].astype(BF16))
    mid = (a * jax.nn.sigmoid(a) * u).astype(BF16)
    y = _dot(mid, wd_ref[0].astype(BF16))
    for j in range(nch):
        y_scr[pl.ds(j, cap, stride=nch), :] = y[:, j * LANE:(j + 1) * LANE]

    for j in range(cap // SCATTER_ROWS):
        base = j * SCATTER_ROWS
        toks = [idx_ref[0, 0, base + r] for r in range(SCATTER_ROWS)]
        rows = [o_ref[0, tile_of(toks[r]), :] for r in range(SCATTER_ROWS)]
        for r in range(SCATTER_ROWS):
            gate = aff_ref[0, 0, toks[r]]
            i = base + r
            o_ref[0, tile_of(toks[r]), :] = rows[r] + gate * y_scr[i * nch:(i + 1) * nch, :]


def _moe(idx, aff, h1, lw, layer, cap):
    b, seq, d = h1.shape
    ne = N_EXPERTS
    ff = EXPERT_FF
    idx3 = idx.reshape(b * ne, 1, cap)
    aff3 = aff.reshape(b * ne, 1, seq)
    once = pl.Buffered(1)
    assert d == SUBLANES * LANE
    rows = seq * SUBLANES
    h1 = h1.reshape(b, rows, LANE)
    return pl.pallas_call(
        functools.partial(_moe_kernel, cap=cap),
        grid=(b, ne),
        in_specs=[pl.BlockSpec((1, 1, cap), lambda bi, e: (bi * ne + e, 0, 0),
                               memory_space=pltpu.SMEM),
                  pl.BlockSpec((1, 1, seq), lambda bi, e: (bi * ne + e, 0, 0),
                               memory_space=pltpu.SMEM),
                  pl.BlockSpec((1, rows, LANE), lambda bi, e: (bi, 0, 0), pipeline_mode=once),
                  pl.BlockSpec((1, d), lambda bi, e: (0, 0)),
                  pl.BlockSpec((None, 1, d, ff), lambda bi, e: (layer, e, 0, 0)),
                  pl.BlockSpec((None, 1, d, ff), lambda bi, e: (layer, e, 0, 0)),
                  pl.BlockSpec((None, 1, ff, d), lambda bi, e: (layer, e, 0, 0))],
        out_specs=pl.BlockSpec((1, rows, LANE), lambda bi, e: (bi, 0, 0), pipeline_mode=once),
        out_shape=jax.ShapeDtypeStruct((b, rows, LANE), F32),
        scratch_shapes=[pltpu.VMEM((cap * SUBLANES, LANE), F32),
                        pltpu.VMEM((cap * SUBLANES, LANE), F32)],
        compiler_params=_cparams(("parallel", "arbitrary")),
        name="ec_moe",
    )(idx3, aff3, h1, lw['g_ffn'], lw['w_gate'], lw['w_up'], lw['w_down'])


def _ple_kernel(h_ref, moe_ref, p_ref, g_ref, wg_ref, wp_ref, gf_ref, o_ref, *, final):
    h = h_ref[...] + moe_ref[...]
    gate = jax.nn.sigmoid(_dot(_rms(h, g_ref[...]).astype(BF16), wg_ref[...]))
    out = h + gate * _dot(p_ref[...].astype(BF16), wp_ref[...])
    if final:
        out = _rms(out, gf_ref[...])
    o_ref[...] = out


def _ple(h, moe, p, lw, g_final, final):
    t, d = h.shape
    tm = TOK_TILE

    def full(a):
        return pl.BlockSpec(a.shape, lambda i: (0,) * a.ndim)

    def tok(w):
        return pl.BlockSpec((tm, w), lambda i: (i, 0))

    weights = [lw['g_ple'], lw['w_ple_gate'], lw['w_ple_proj'], g_final]
    return pl.pallas_call(
        functools.partial(_ple_kernel, final=final),
        grid=(t // tm,),
        in_specs=[tok(d), tok(d), tok(p.shape[1])] + [full(w) for w in weights],
        out_specs=tok(d),
        out_shape=jax.ShapeDtypeStruct((t, d), F32),
        compiler_params=_cparams(("parallel",)),
        name="ple",
    )(h, moe, p, *weights)


def _rot_cols(w, d):
    g = w.reshape(w.shape[:-1] + (w.shape[-1] // d, d))
    return jnp.concatenate([-g[..., d // 2:], g[..., :d // 2]], axis=-1).reshape(w.shape)


def _rope_tables(seq):
    pos = jnp.arange(seq, dtype=F32)

    def cs(dim):
        inv = 1.0 / (ROPE_THETA ** (jnp.arange(0, dim, 2, dtype=F32) / dim))
        ang = pos[:, None] * inv[None, :]
        return jnp.cos(ang), jnp.sin(ang)

    cm, sm = cs(MLA_ROPE_DIM)
    cw, sw = cs(SWA_HEAD_DIM)
    z = lambda n: jnp.zeros((seq, n), F32)
    one = jnp.ones((seq, MLA_NOPE_DIM), F32)
    pad = MLA_HEAD_PAD - MLA_NOPE_DIM - MLA_ROPE_DIM
    qs = (MLA_NOPE_DIM + MLA_ROPE_DIM) ** -0.5 * math.log2(math.e)
    return {
        'cq': qs * jnp.concatenate([one, cm, cm, z(pad)], axis=1),
        'sq': qs * jnp.concatenate([z(MLA_NOPE_DIM), sm, sm, z(pad)], axis=1),
        'ck': jnp.concatenate([z(MLA_NOPE_DIM), cm, cm, z(pad)], axis=1),
        'sk': jnp.concatenate([z(MLA_NOPE_DIM), sm, sm, z(pad)], axis=1),
        'cs': jnp.concatenate([cw, cw] * (LANE // SWA_HEAD_DIM), axis=1),
        'ss': jnp.concatenate([sw, sw] * (LANE // SWA_HEAD_DIM), axis=1),
    }


def _na_bias_tables(rpb):
    nh = rpb.shape[0]
    qi = jnp.arange(NA_QROWS)[:, None]
    ki = jnp.arange(NA_KROWS)[None, :]
    half = NA_WIN_ROWS // 2
    cases = [
        ((ki < NA_WIN_ROWS) & (qi >= 0), ki - qi),
        ((ki >= qi) & (ki < qi + NA_WIN_ROWS), ki - half - qi),
        ((ki >= NA_KROWS - NA_WIN_ROWS) & (qi >= 0), ki - (NA_KROWS - NA_QROWS) - qi),
    ]
    row_ok = jnp.stack([ok for ok, _ in cases])
    dri = jnp.stack([jnp.clip(dr + NA_WIN_ROWS - 1, 0, 2 * NA_WIN_ROWS - 2) for _, dr in cases])
    c = jnp.arange(GRID_W)[:, None]
    kc = jnp.arange(GRID_W)[None, :]
    c0 = jnp.clip(c - NA_WIN_COLS // 2, 0, GRID_W - NA_WIN_COLS)
    col_ok = (kc >= c0) & (kc < c0 + NA_WIN_COLS)
    dci = jnp.clip(kc - c, -(NA_WIN_COLS - 1), NA_WIN_COLS - 1) + NA_WIN_COLS - 1
    one_r = jax.nn.one_hot(dri, 2 * NA_WIN_ROWS - 1, dtype=F32)
    one_c = jax.nn.one_hot(dci, 2 * NA_WIN_COLS - 1, dtype=F32)
    bias = jnp.einsum('xqka,hab,cnb->xhqckn', one_r, rpb.astype(F32), one_c,
                      precision=lax.Precision.HIGHEST)
    ok = row_ok[:, None, :, None, :, None] & col_ok[None, None, None, :, None, :]
    tab = jnp.where(ok, bias, NEG_INF)
    return tab.reshape(3, nh, NA_QROWS * GRID_W, NA_KROWS * GRID_W)


def _layer_weights(i, attn_norm, w_in, mla_q_norm, mla_w_q_up, mla_kv_norm, mla_w_kv_up, w_out,
                   ffn_norm, w_router, w_gate_e, w_up_e, w_down_e, ple_norm, w_ple_gate,
                   w_ple_proj):
    offs = [0]
    for n in IN_SIZES:
        offs.append(offs[-1] + n)
    parts = [w_in[i][:, offs[j]:offs[j + 1]] for j in range(len(IN_SIZES))]
    na_q, na_k, na_v, m_cq, m_ckv, m_kr, s_q, s_k, s_v = parts
    d = w_in.shape[1]
    na_scale = NA_HEAD_DIM ** -0.5
    swa_scale = SWA_HEAD_DIM ** -0.5

    zk = jnp.zeros((d, MLA_NOPE_DIM), F32)
    zp = jnp.zeros((d, MLA_HEAD_PAD - MLA_NOPE_DIM - MLA_ROPE_DIM), F32)
    w_kr = jnp.concatenate([zk, m_kr, zp, zk, _rot_cols(m_kr, MLA_ROPE_DIM), zp], axis=1)

    qu = mla_w_q_up[i].reshape(MLA_Q_RANK, MLA_HEADS, MLA_NOPE_DIM + MLA_ROPE_DIM)
    q_nope, q_rope = qu[..., :MLA_NOPE_DIM], qu[..., MLA_NOPE_DIM:]
    zq = jnp.zeros_like(q_rope)
    q_main = jnp.concatenate([q_nope, q_rope, zq], axis=-1)
    q_rot = jnp.concatenate([jnp.zeros_like(q_nope), _rot_cols(q_rope, MLA_ROPE_DIM), zq], axis=-1)
    w_q_up = jnp.concatenate([q_main.reshape(MLA_Q_RANK, -1), q_rot.reshape(MLA_Q_RANK, -1)], axis=1)

    kvu = mla_w_kv_up[i].reshape(MLA_KV_RANK, MLA_HEADS, MLA_NOPE_DIM + MLA_V_DIM)
    k_nope, v_up = kvu[..., :MLA_NOPE_DIM], kvu[..., MLA_NOPE_DIM:]
    zv = jnp.zeros_like(v_up)
    k_main = jnp.concatenate([k_nope, jnp.zeros_like(k_nope)], axis=-1)
    even = (jnp.arange(MLA_HEADS) % 2 == 0)[None, :, None]
    v_main = jnp.concatenate([jnp.where(even, v_up, zv), jnp.where(even, zv, v_up)], axis=-1)
    w_kv_up = jnp.concatenate([k_main.reshape(MLA_KV_RANK, -1), v_main.reshape(MLA_KV_RANK, -1)],
                              axis=1)

    wr = w_router[i].T
    wr_hi = wr.astype(BF16)
    wr_lo = (wr - wr_hi.astype(F32)).astype(BF16)
    wo = w_out[i]
    row = lambda g: g.reshape(1, -1).astype(F32)
    return {
        'g_attn': row(attn_norm[i]), 'g_q': row(mla_q_norm[i]), 'g_kv': row(mla_kv_norm[i]),
        'w_na': jnp.concatenate([na_q * na_scale, na_k, na_v], axis=1).astype(BF16),
        'w_mc': jnp.concatenate([m_cq, m_ckv], axis=1).astype(BF16),
        'w_kr': w_kr.astype(BF16),
        'w_swa': jnp.concatenate([s_q * swa_scale, _rot_cols(s_q, SWA_HEAD_DIM) * swa_scale,
                                  s_k, _rot_cols(s_k, SWA_HEAD_DIM), s_v], axis=1).astype(BF16),
        'w_q_up': w_q_up.astype(BF16), 'w_kv_up': w_kv_up.astype(BF16),
        'w_out_na': wo[:NA_WIDTH].astype(BF16),
        'w_out_mla': wo[NA_WIDTH:NA_WIDTH + MLA_WIDTH].astype(BF16),
        'w_out_swa': wo[NA_WIDTH + MLA_WIDTH:].astype(BF16),
        'g_ffn': row(ffn_norm[i]),
        'w_router': jnp.concatenate([wr_hi, wr_lo], axis=0),
        'w_gate': w_gate_e, 'w_up': w_up_e, 'w_down': w_down_e,
        'g_ple': row(ple_norm[i]),
        'w_ple_gate': w_ple_gate[i].astype(BF16), 'w_ple_proj': w_ple_proj[i].astype(BF16),
    }


def kernel(x, p, attn_norm, w_in, na_rpb, mla_q_norm, mla_w_q_up, mla_kv_norm, mla_w_kv_up,
           swa_sink, w_out, ffn_norm, w_router, w_gate_e, w_up_e, w_down_e, ple_norm, w_ple_gate,
           w_ple_proj, final_norm):
    b, seq, d = x.shape
    depth = w_in.shape[0]
    t = b * seq
    cap = EC_CAPACITY_FACTOR * seq // N_EXPERTS
    tabs = _rope_tables(seq)
    tri = jnp.triu(jnp.ones((LANE, LANE), F32)).astype(BF16)
    g_final = final_norm.reshape(1, -1).astype(F32)

    h = x.reshape(t, d)
    for i in range(depth):
        lw = _layer_weights(i, attn_norm, w_in, mla_q_norm, mla_w_q_up, mla_kv_norm, mla_w_kv_up,
                            w_out, ffn_norm, w_router, w_gate_e, w_up_e, w_down_e, ple_norm,
                            w_ple_gate, w_ple_proj)
        (na_q, na_k, na_v, m_q, m_k, m_v, s_q, s_k, s_v) = _inproj(h, lw, tabs, seq)
        r3 = lambda a: a.reshape(b, seq, a.shape[-1])
        o_na = _na_attention(r3(na_q), r3(na_k), r3(na_v), _na_bias_tables(na_rpb[i]))
        o_mla = _mla_attention(r3(m_q), r3(m_k), r3(m_v))
        o_swa = _swa_attention(r3(s_q), r3(s_k), r3(s_v), swa_sink[i].astype(F32))
        r2 = lambda a: a.reshape(t, a.shape[-1])
        h1, aff = _outproj(h, r2(o_na), r2(o_mla), r2(o_swa), lw, b, seq)
        idx = _select(aff, tri, cap).reshape(b, N_EXPERTS, cap)
        moe = _moe(idx, aff, h1.reshape(b, seq, d), lw, i, cap)
        h = _ple(h1, moe.reshape(t, d), p[i].reshape(t, PLE_DIM), lw, g_final, i == depth - 1)
    return h.reshape(b, seq, d)
```

```python
import functools
import math

import jax
import jax.numpy as jnp
from jax import lax
from jax.experimental import pallas as pl
from jax.experimental.pallas import tpu as pltpu

F32 = jnp.float32
BF16 = jnp.bfloat16

GRID_W = 64
NA_HEADS = 4
NA_HEAD_DIM = 64
NA_WIN_ROWS = 8
NA_WIN_COLS = 16
MLA_HEADS = 8
MLA_Q_RANK = 384
MLA_KV_RANK = 256
MLA_NOPE_DIM = 64
MLA_ROPE_DIM = 32
MLA_V_DIM = 64
SWA_HEADS = 4
SWA_KV_HEADS = 2
SWA_HEAD_DIM = 64
SWA_WINDOW = 128
SWA_BLOCK = 128
N_EXPERTS = 16
EXPERT_FF = 512
EC_CAPACITY_FACTOR = 2
PLE_DIM = 256
ROPE_THETA = 10000.0
NORM_EPS = 1e-6
NEG_INF = -1e30

NA_WIDTH = NA_HEADS * NA_HEAD_DIM
MLA_WIDTH = MLA_HEADS * MLA_V_DIM
SWA_WIDTH = SWA_HEADS * SWA_HEAD_DIM
SWA_KV_WIDTH = SWA_KV_HEADS * SWA_HEAD_DIM
IN_SIZES = (NA_WIDTH, NA_WIDTH, NA_WIDTH, MLA_Q_RANK, MLA_KV_RANK, MLA_ROPE_DIM,
            SWA_WIDTH, SWA_KV_WIDTH, SWA_KV_WIDTH)

LANE = 128
SUBLANES = 8
MLA_HEAD_PAD = LANE
VMEM_LIMIT = 56 * 1024 * 1024

TOK_TILE = 512
NA_QROWS = 4
NA_KROWS = 12
NA_QBLOCKS = 4
SWA_QBLOCKS = 4
MLA_TQ = 256
MLA_HEADS_PER_STEP = 8


def _cparams(sem):
    return pltpu.CompilerParams(dimension_semantics=sem, vmem_limit_bytes=VMEM_LIMIT)


def _dot(a, b):
    return jnp.dot(a, b, preferred_element_type=F32)


def _dot_t(a, b):
    return lax.dot_general(a, b, (((1,), (1,)), ((), ())), preferred_element_type=F32)


def _rms(x, g):
    return x * lax.rsqrt(jnp.mean(x * x, axis=-1, keepdims=True) + NORM_EPS) * g


def _inproj_kernel(h_ref, g_ref, gq_ref, gkv_ref, wna_ref, wmc_ref, wkr_ref, wswa_ref,
                   wq_ref, wkv_ref, cq_ref, sq_ref, ck_ref, sk_ref, cs_ref, ss_ref,
                   naq_ref, nak_ref, nav_ref, mq_ref, mk_ref, mv_ref,
                   swq_ref, swk_ref, swv_ref):
    xn = _rms(h_ref[...], g_ref[...]).astype(BF16)

    pna = _dot(xn, wna_ref[...])
    naq_ref[...] = pna[:, :NA_WIDTH].astype(BF16)
    nak_ref[...] = pna[:, NA_WIDTH:2 * NA_WIDTH].astype(BF16)
    nav_ref[...] = pna[:, 2 * NA_WIDTH:].astype(BF16)

    pmc = _dot(xn, wmc_ref[...])
    cqn = _rms(pmc[:, :MLA_Q_RANK], gq_ref[...]).astype(BF16)
    ckvn = _rms(pmc[:, MLA_Q_RANK:], gkv_ref[...]).astype(BF16)
    q2 = _dot(cqn, wq_ref[...])
    kv2 = _dot(ckvn, wkv_ref[...])
    pkr = _dot(xn, wkr_ref[...])
    kr = pkr[:, :LANE] * ck_ref[...] + pkr[:, LANE:] * sk_ref[...]
    hw = MLA_HEADS * MLA_HEAD_PAD
    cq = cq_ref[...]
    sq = sq_ref[...]
    for hd in range(MLA_HEADS):
        lo, hi = hd * MLA_HEAD_PAD, (hd + 1) * MLA_HEAD_PAD
        mq_ref[:, lo:hi] = (q2[:, lo:hi] * cq + q2[:, hw + lo:hw + hi] * sq).astype(BF16)
        mk_ref[:, lo:hi] = (kv2[:, lo:hi] + kr).astype(BF16)
    mv_ref[...] = kv2[:, hw:].astype(BF16)

    ps = _dot(xn, wswa_ref[...])
    cs = cs_ref[...]
    ss = ss_ref[...]
    for j in range(SWA_WIDTH // LANE):
        lo, hi = j * LANE, (j + 1) * LANE
        swq_ref[:, lo:hi] = (ps[:, lo:hi] * cs
                             + ps[:, SWA_WIDTH + lo:SWA_WIDTH + hi] * ss).astype(BF16)
    o = 2 * SWA_WIDTH
    swk_ref[...] = (ps[:, o:o + SWA_KV_WIDTH] * cs
                    + ps[:, o + SWA_KV_WIDTH:o + 2 * SWA_KV_WIDTH] * ss).astype(BF16)
    swv_ref[...] = ps[:, o + 2 * SWA_KV_WIDTH:].astype(BF16)


def _inproj(h, lw, tabs, seq):
    t, d = h.shape
    tm = TOK_TILE
    nt = t // tm
    spt = seq // tm

    def full(a):
        return pl.BlockSpec(a.shape, lambda i: (0,) * a.ndim)

    def tab(a):
        return pl.BlockSpec((tm, a.shape[1]), lambda i: (i % spt, 0))

    def tok(w):
        return pl.BlockSpec((tm, w), lambda i: (i, 0))

    weights = [lw['g_attn'], lw['g_q'], lw['g_kv'], lw['w_na'], lw['w_mc'], lw['w_kr'],
               lw['w_swa'], lw['w_q_up'], lw['w_kv_up']]
    tables = [tabs['cq'], tabs['sq'], tabs['ck'], tabs['sk'], tabs['cs'], tabs['ss']]
    widths = [NA_WIDTH, NA_WIDTH, NA_WIDTH, MLA_HEADS * MLA_HEAD_PAD, MLA_HEADS * MLA_HEAD_PAD,
              MLA_HEADS * MLA_HEAD_PAD, SWA_WIDTH, SWA_KV_WIDTH, SWA_KV_WIDTH]
    return pl.pallas_call(
        _inproj_kernel,
        grid=(nt,),
        in_specs=[tok(d)] + [full(w) for w in weights] + [tab(a) for a in tables],
        out_specs=[tok(w) for w in widths],
        out_shape=[jax.ShapeDtypeStruct((t, w), BF16) for w in widths],
        compiler_params=_cparams(("parallel",)),
        name="inproj",
    )(h, *weights, *tables)


def _na_kernel(q_ref, k_ref, v_ref, tbl_ref, o_ref, *, n_blocks):
    last_start = (n_blocks * NA_QROWS - NA_KROWS)
    nk = NA_KROWS * GRID_W
    tq = NA_QROWS * GRID_W
    for sub in range(NA_QBLOCKS):
        blk = pl.program_id(1) * NA_QBLOCKS + sub
        ks = jnp.clip(blk * NA_QROWS - NA_WIN_ROWS // 2, 0, last_start) * GRID_W
        ks = pl.multiple_of(ks, GRID_W)
        sel = jnp.where(blk == 0, 0, jnp.where(blk == n_blocks - 1, 2, 1))
        outs = []
        for hd in range(NA_HEADS):
            lo, hi = hd * NA_HEAD_DIM, (hd + 1) * NA_HEAD_DIM
            q = q_ref[0, sub * tq:(sub + 1) * tq, lo:hi]
            k = k_ref[0, pl.ds(ks, nk), lo:hi]
            v = v_ref[0, pl.ds(ks, nk), lo:hi]
            s = _dot_t(q, k) + tbl_ref[sel, hd]
            m = jnp.max(s, axis=-1, keepdims=True)
            e = jnp.exp(s - m)
            l = jnp.sum(e, axis=-1, keepdims=True)
            outs.append(_dot(e.astype(BF16), v) / l)
        o_ref[0, sub * tq:(sub + 1) * tq, :] = jnp.concatenate(outs, axis=-1).astype(BF16)


def _na_attention(q, k, v, tbl):
    b, s, w = q.shape
    rows = s // GRID_W
    nb = rows // NA_QROWS
    tq = NA_QROWS * GRID_W * NA_QBLOCKS
    return pl.pallas_call(
        functools.partial(_na_kernel, n_blocks=nb),
        grid=(b, nb // NA_QBLOCKS),
        in_specs=[pl.BlockSpec((1, tq, w), lambda bi, blk: (bi, blk, 0)),
                  pl.BlockSpec((1, s, w), lambda bi, blk: (bi, 0, 0)),
                  pl.BlockSpec((1, s, w), lambda bi, blk: (bi, 0, 0)),
                  pl.BlockSpec(tbl.shape, lambda bi, blk: (0, 0, 0, 0))],
        out_specs=pl.BlockSpec((1, tq, w), lambda bi, blk: (bi, blk, 0)),
        out_shape=jax.ShapeDtypeStruct((b, s, w), BF16),
        compiler_params=_cparams(("parallel", "arbitrary")),
        name="na_attn",
    )(q, k, v, tbl)


def _mla_kernel(q_ref, k_ref, v_ref, o_ref):
    for pair in range(MLA_HEADS_PER_STEP // 2):
        acc = None
        for hh in range(2 * pair, 2 * pair + 2):
            lo, hi = hh * MLA_HEAD_PAD, (hh + 1) * MLA_HEAD_PAD
            q = q_ref[0, :, lo:hi]
            k = k_ref[0, :, lo:hi]
            v = v_ref[0, :, lo:hi]
            s = _dot_t(q, k)
            m = jnp.max(s, axis=-1, keepdims=True)
            e = jnp.exp2(s - m)
            l = jnp.sum(e, axis=-1, keepdims=True)
            o = _dot(e.astype(BF16), v) / l
            acc = o if acc is None else acc + o
        o_ref[0, :, pair * LANE:(pair + 1) * LANE] = acc.astype(BF16)


def _mla_attention(q, k, v):
    b, s, _ = q.shape
    hps = MLA_HEADS_PER_STEP
    pw = hps * MLA_HEAD_PAD
    ow = hps * MLA_V_DIM
    ngrp = MLA_HEADS // hps
    nq = s // MLA_TQ
    return pl.pallas_call(
        _mla_kernel,
        grid=(b, ngrp, nq),
        in_specs=[pl.BlockSpec((1, MLA_TQ, pw), lambda bi, hg, qi: (bi, qi, hg)),
                  pl.BlockSpec((1, s, pw), lambda bi, hg, qi: (bi, 0, hg)),
                  pl.BlockSpec((1, s, pw), lambda bi, hg, qi: (bi, 0, hg))],
        out_specs=pl.BlockSpec((1, MLA_TQ, ow), lambda bi, hg, qi: (bi, qi, hg)),
        out_shape=jax.ShapeDtypeStruct((b, s, MLA_WIDTH), BF16),
        compiler_params=_cparams(("parallel", "parallel", "arbitrary")),
        name="mla_attn",
    )(q, k, v)


def _swa_kernel(sink_ref, q_ref, k_ref, v_ref, o_ref, *, seq):
    blk = SWA_BLOCK
    nk = 3 * blk
    group = SWA_HEADS // SWA_KV_HEADS
    rows = group * blk
    ri = lax.broadcasted_iota(jnp.int32, (rows, nk), 0)
    ci = lax.broadcasted_iota(jnp.int32, (rows, nk), 1)
    rcol = lax.broadcasted_iota(jnp.int32, (rows, 1), 0)
    for sub in range(SWA_QBLOCKS):
        n = pl.program_id(1) * SWA_QBLOCKS + sub
        ks = pl.multiple_of(jnp.clip((n - 1) * blk, 0, seq - nk), blk)
        rel = (ks + ci) - (n * blk + (ri & (blk - 1)))
        valid = jnp.abs(rel) <= SWA_WINDOW
        outs = []
        for kvh in range(SWA_KV_HEADS):
            qs = []
            sink = jnp.zeros((rows, 1), F32)
            for g in range(group):
                hd = kvh * group + g
                qs.append(q_ref[0, sub * blk:(sub + 1) * blk,
                                hd * SWA_HEAD_DIM:(hd + 1) * SWA_HEAD_DIM])
                sink = jnp.where((rcol >= g * blk) & (rcol < (g + 1) * blk), sink_ref[hd], sink)
            q = jnp.concatenate(qs, axis=0)
            lo, hi = kvh * SWA_HEAD_DIM, (kvh + 1) * SWA_HEAD_DIM
            k = k_ref[0, pl.ds(ks, nk), lo:hi]
            v = v_ref[0, pl.ds(ks, nk), lo:hi]
            s = jnp.where(valid, _dot_t(q, k), NEG_INF)
            m = jnp.maximum(jnp.max(s, axis=-1, keepdims=True), sink)
            e = jnp.exp(s - m)
            den = jnp.sum(e, axis=-1, keepdims=True) + jnp.exp(sink - m)
            o = _dot(e.astype(BF16), v) / den
            for g in range(group):
                outs.append(o[g * blk:(g + 1) * blk])
        o_ref[0, sub * blk:(sub + 1) * blk, :] = jnp.concatenate(outs, axis=-1).astype(BF16)


def _swa_attention(q, k, v, sink):
    b, s, w = q.shape
    kw = k.shape[2]
    tq = SWA_BLOCK * SWA_QBLOCKS
    return pl.pallas_call(
        functools.partial(_swa_kernel, seq=s),
        grid=(b, s // tq),
        in_specs=[pl.BlockSpec(memory_space=pltpu.SMEM),
                  pl.BlockSpec((1, tq, w), lambda bi, n: (bi, n, 0)),
                  pl.BlockSpec((1, s, kw), lambda bi, n: (bi, 0, 0)),
                  pl.BlockSpec((1, s, kw), lambda bi, n: (bi, 0, 0))],
        out_specs=pl.BlockSpec((1, tq, w), lambda bi, n: (bi, n, 0)),
        out_shape=jax.ShapeDtypeStruct((b, s, w), BF16),
        compiler_params=_cparams(("parallel", "arbitrary")),
        name="swa_attn",
    )(sink, q, k, v)


def _outproj_kernel(h_ref, ona_ref, omla_ref, oswa_ref, wna_ref, wmla_ref, wswa_ref,
                    g_ref, wr_ref, h1_ref, h1t_ref, aff_ref):
    h1 = (h_ref[...] + _dot(ona_ref[...], wna_ref[...]) + _dot(omla_ref[...], wmla_ref[...])
          + _dot(oswa_ref[...], wswa_ref[...]))
    h1_ref[...] = h1
    tm = h1.shape[0]
    for j in range(SUBLANES):
        h1t_ref[pl.ds(j, tm, stride=SUBLANES), :] = h1[:, j * LANE:(j + 1) * LANE]
    hn = _rms(h1, g_ref[...])
    hn_hi = hn.astype(BF16)
    hn_lo = (hn - hn_hi.astype(F32)).astype(BF16)
    wr = wr_ref[...]
    p_hi = _dot_t(wr, hn_hi)
    p_lo = _dot_t(wr, hn_lo)
    logits = p_hi[:N_EXPERTS] + p_hi[N_EXPERTS:] + p_lo[:N_EXPERTS]
    m = jnp.max(logits, axis=0, keepdims=True)
    e = jnp.exp(logits - m)
    aff_ref[0] = e / jnp.sum(e, axis=0, keepdims=True)


def _outproj(h, o_na, o_mla, o_swa, lw, batch, seq):
    t, d = h.shape
    tm = TOK_TILE
    spt = seq // tm

    def full(a):
        return pl.BlockSpec(a.shape, lambda i: (0,) * a.ndim)

    def tok(w):
        return pl.BlockSpec((tm, w), lambda i: (i, 0))

    weights = [lw['w_out_na'], lw['w_out_mla'], lw['w_out_swa'], lw['g_ffn'], lw['w_router']]
    return pl.pallas_call(
        _outproj_kernel,
        grid=(t // tm,),
        in_specs=[tok(d), tok(NA_WIDTH), tok(MLA_WIDTH), tok(SWA_WIDTH)] + [full(w) for w in weights],
        out_specs=[tok(d), pl.BlockSpec((tm * SUBLANES, LANE), lambda i: (i, 0)),
                   pl.BlockSpec((1, N_EXPERTS, tm), lambda i: (i // spt, 0, i % spt))],
        out_shape=[jax.ShapeDtypeStruct((t, d), F32),
                   jax.ShapeDtypeStruct((t * SUBLANES, LANE), F32),
                   jax.ShapeDtypeStruct((batch, N_EXPERTS, seq), F32)],
        compiler_params=_cparams(("parallel",)),
        name="outproj_router",
    )(h, o_na, o_mla, o_swa, *weights)


def _select_kernel(aff_ref, tri_ref, idx_ref, csum_scr, *, cap):
    x = aff_ref[0]
    ne, seq = x.shape
    capf = float(cap)

    def count_ge(v):
        return jnp.sum(jnp.where(x >= v, 1.0, 0.0), axis=1, keepdims=True)

    def search(i, tb):
        cand = tb | jnp.left_shift(jnp.int32(1), 30 - i)
        return jnp.where(count_ge(pltpu.bitcast(cand, F32)) >= capf, cand, tb)

    tb = lax.fori_loop(0, 31, search, jnp.zeros((ne, 1), jnp.int32))

    def refine(i, lohi):
        lo, hi = lohi
        mid = 0.5 * (lo + hi)
        ok = count_ge(mid) >= capf
        return jnp.where(ok, mid, lo), jnp.where(ok, hi, mid)

    thr, _ = lax.fori_loop(0, 24, refine, (pltpu.bitcast(tb, F32), pltpu.bitcast(tb + 1, F32)))
    gt = jnp.where(x > thr, 1.0, 0.0)
    eq = jnp.where(x == thr, 1.0, 0.0)
    need = capf - jnp.sum(gt, axis=1, keepdims=True)
    tri = tri_ref[...]

    def cumsum(mask):
        outs = []
        off = jnp.zeros((ne, 1), F32)
        for c in range(seq // LANE):
            cs = _dot(mask[:, c * LANE:(c + 1) * LANE].astype(BF16), tri) + off
            outs.append(cs)
            off = cs[:, LANE - 1:LANE]
        return jnp.concatenate(outs, axis=1)

    tie_rank = cumsum(eq) - eq
    sel = gt + eq * jnp.where(tie_rank < need, 1.0, 0.0)
    csum = cumsum(sel)
    for e in range(ne):
        csum_scr[e] = csum[e:e + 1, :]

    n_chunks = seq // LANE
    slot_blocks = cap // LANE

    def per_expert(e, carry):
        for sb in range(slot_blocks):
            slot = (lax.broadcasted_iota(jnp.int32, (LANE, LANE), 0) + sb * LANE).astype(F32)

            def chunk(c, acc):
                off = pl.multiple_of(c * LANE, LANE)
                return acc + jnp.where(csum_scr[e, :, pl.ds(off, LANE)] <= slot, 1.0, 0.0)

            acc = lax.fori_loop(0, n_chunks, chunk, jnp.zeros((LANE, LANE), F32), unroll=4)
            idx_ref[0, e, sb * LANE:(sb + 1) * LANE, :] = (
                jnp.sum(acc, axis=1, keepdims=True).astype(jnp.int32))
        return carry

    lax.fori_loop(0, ne, per_expert, 0)


def _select(aff, tri, cap):
    b, ne, seq = aff.shape
    return pl.pallas_call(
        functools.partial(_select_kernel, cap=cap),
        grid=(b,),
        in_specs=[pl.BlockSpec((1, ne, seq), lambda bi: (bi, 0, 0)),
                  pl.BlockSpec(tri.shape, lambda bi: (0, 0))],
        out_specs=pl.BlockSpec((1, ne, cap, 1), lambda bi: (bi, 0, 0, 0)),
        out_shape=jax.ShapeDtypeStruct((b, ne, cap, 1), jnp.int32),
        scratch_shapes=[pltpu.VMEM((ne, 1, seq), F32)],
        compiler_params=_cparams(("parallel",)),
        name="ec_select",
    )(aff, tri)


SCATTER_ROWS = 8


def _moe_kernel(idx_ref, aff_ref, h_ref, g_ref, wg_ref, wu_ref, wd_ref, o_ref, xs_scr, y_scr, *, cap):
    nch = SUBLANES
    @pl.when(pl.program_id(1) == 0)
    def _():
        o_ref[...] = jnp.zeros_like(o_ref)

    def tile_of(tok):
        return pl.ds(pl.multiple_of(tok * nch, nch), nch)

    for i in range(cap):
        xs_scr[i * nch:(i + 1) * nch, :] = h_ref[0, tile_of(idx_ref[0, 0, i]), :]

    xs = jnp.concatenate([xs_scr[pl.ds(j, cap, stride=nch), :] for j in range(nch)], axis=1)
    xn = _rms(xs, g_ref[...]).astype(BF16)
    a = _dot(xn, wg_ref[0].astype(BF16))
    u = _dot(xn, wu_ref[0].astype(BF16))
    mid = (a * jax.nn.sigmoid(a) * u).astype(BF16)
    y = _dot(mid, wd_ref[0].astype(BF16))
    for j in range(nch):
        y_scr[pl.ds(j, cap, stride=nch), :] = y[:, j * LANE:(j + 1) * LANE]

    for j in range(cap // SCATTER_ROWS):
        base = j * SCATTER_ROWS
        toks = [idx_ref[0, 0, base + r] for r in range(SCATTER_ROWS)]
        rows = [o_ref[0, tile_of(toks[r]), :] for r in range(SCATTER_ROWS)]
        for r in range(SCATTER_ROWS):
            gate = aff_ref[0, 0, toks[r]]
            i = base + r
            o_ref[0, tile_of(toks[r]), :] = rows[r] + gate * y_scr[i * nch:(i + 1) * nch, :]


def _moe(idx, aff, h1t, lw, layer, cap):
    b, rows, _ = h1t.shape
    seq = rows // SUBLANES
    d = SUBLANES * LANE
    ne = N_EXPERTS
    ff = EXPERT_FF
    idx3 = idx.reshape(b * ne, 1, cap)
    aff3 = aff.reshape(b * ne, 1, seq)
    once = pl.Buffered(1)
    return pl.pallas_call(
        functools.partial(_moe_kernel, cap=cap),
        grid=(b, ne),
        in_specs=[pl.BlockSpec((1, 1, cap), lambda bi, e: (bi * ne + e, 0, 0),
                               memory_space=pltpu.SMEM),
                  pl.BlockSpec((1, 1, seq), lambda bi, e: (bi * ne + e, 0, 0),
                               memory_space=pltpu.SMEM),
                  pl.BlockSpec((1, rows, LANE), lambda bi, e: (bi, 0, 0), pipeline_mode=once),
                  pl.BlockSpec((1, d), lambda bi, e: (0, 0)),
                  pl.BlockSpec((None, 1, d, ff), lambda bi, e: (layer, e, 0, 0)),
                  pl.BlockSpec((None, 1, d, ff), lambda bi, e: (layer, e, 0, 0)),
                  pl.BlockSpec((None, 1, ff, d), lambda bi, e: (layer, e, 0, 0))],
        out_specs=pl.BlockSpec((1, rows, LANE), lambda bi, e: (bi, 0, 0), pipeline_mode=once),
        out_shape=jax.ShapeDtypeStruct((b, rows, LANE), F32),
        scratch_shapes=[pltpu.VMEM((cap * SUBLANES, LANE), F32),
                        pltpu.VMEM((cap * SUBLANES, LANE), F32)],
        compiler_params=_cparams(("parallel", "arbitrary")),
        name="ec_moe",
    )(idx3, aff3, h1t, lw['g_ffn'], lw['w_gate'], lw['w_up'], lw['w_down'])


def _ple_kernel(h_ref, moe_ref, p_ref, g_ref, wg_ref, wp_ref, gf_ref, o_ref, *, final):
    tm = h_ref.shape[0]
    moe = jnp.concatenate([moe_ref[pl.ds(j, tm, stride=SUBLANES), :] for j in range(SUBLANES)],
                          axis=1)
    h = h_ref[...] + moe
    gate = jax.nn.sigmoid(_dot(_rms(h, g_ref[...]).astype(BF16), wg_ref[...]))
    out = h + gate * _dot(p_ref[...].astype(BF16), wp_ref[...])
    if final:
        out = _rms(out, gf_ref[...])
    o_ref[...] = out


def _ple(h, moe, p, lw, g_final, final):
    t, d = h.shape
    tm = TOK_TILE

    def full(a):
        return pl.BlockSpec(a.shape, lambda i: (0,) * a.ndim)

    def tok(w):
        return pl.BlockSpec((tm, w), lambda i: (i, 0))

    weights = [lw['g_ple'], lw['w_ple_gate'], lw['w_ple_proj'], g_final]
    return pl.pallas_call(
        functools.partial(_ple_kernel, final=final),
        grid=(t // tm,),
        in_specs=[tok(d), pl.BlockSpec((tm * SUBLANES, LANE), lambda i: (i, 0)), tok(p.shape[1])]
                 + [full(w) for w in weights],
        out_specs=tok(d),
        out_shape=jax.ShapeDtypeStruct((t, d), F32),
        compiler_params=_cparams(("parallel",)),
        name="ple",
    )(h, moe, p, *weights)


def _rot_cols(w, d):
    g = w.reshape(w.shape[:-1] + (w.shape[-1] // d, d))
    return jnp.concatenate([-g[..., d // 2:], g[..., :d // 2]], axis=-1).reshape(w.shape)


def _rope_tables(seq):
    pos = jnp.arange(seq, dtype=F32)

    def cs(dim):
        inv = 1.0 / (ROPE_THETA ** (jnp.arange(0, dim, 2, dtype=F32) / dim))
        ang = pos[:, None] * inv[None, :]
        return jnp.cos(ang), jnp.sin(ang)

    cm, sm = cs(MLA_ROPE_DIM)
    cw, sw = cs(SWA_HEAD_DIM)
    z = lambda n: jnp.zeros((seq, n), F32)
    one = jnp.ones((seq, MLA_NOPE_DIM), F32)
    pad = MLA_HEAD_PAD - MLA_NOPE_DIM - MLA_ROPE_DIM
    qs = (MLA_NOPE_DIM + MLA_ROPE_DIM) ** -0.5 * math.log2(math.e)
    return {
        'cq': qs * jnp.concatenate([one, cm, cm, z(pad)], axis=1),
        'sq': qs * jnp.concatenate([z(MLA_NOPE_DIM), sm, sm, z(pad)], axis=1),
        'ck': jnp.concatenate([z(MLA_NOPE_DIM), cm, cm, z(pad)], axis=1),
        'sk': jnp.concatenate([z(MLA_NOPE_DIM), sm, sm, z(pad)], axis=1),
        'cs': jnp.concatenate([cw, cw] * (LANE // SWA_HEAD_DIM), axis=1),
        'ss': jnp.concatenate([sw, sw] * (LANE // SWA_HEAD_DIM), axis=1),
    }


def _na_bias_tables(rpb):
    nh = rpb.shape[0]
    qi = jnp.arange(NA_QROWS)[:, None]
    ki = jnp.arange(NA_KROWS)[None, :]
    half = NA_WIN_ROWS // 2
    cases = [
        ((ki < NA_WIN_ROWS) & (qi >= 0), ki - qi),
        ((ki >= qi) & (ki < qi + NA_WIN_ROWS), ki - half - qi),
        ((ki >= NA_KROWS - NA_WIN_ROWS) & (qi >= 0), ki - (NA_KROWS - NA_QROWS) - qi),
    ]
    row_ok = jnp.stack([ok for ok, _ in cases])
    dri = jnp.stack([jnp.clip(dr + NA_WIN_ROWS - 1, 0, 2 * NA_WIN_ROWS - 2) for _, dr in cases])
    c = jnp.arange(GRID_W)[:, None]
    kc = jnp.arange(GRID_W)[None, :]
    c0 = jnp.clip(c - NA_WIN_COLS // 2, 0, GRID_W - NA_WIN_COLS)
    col_ok = (kc >= c0) & (kc < c0 + NA_WIN_COLS)
    dci = jnp.clip(kc - c, -(NA_WIN_COLS - 1), NA_WIN_COLS - 1) + NA_WIN_COLS - 1
    one_r = jax.nn.one_hot(dri, 2 * NA_WIN_ROWS - 1, dtype=F32)
    one_c = jax.nn.one_hot(dci, 2 * NA_WIN_COLS - 1, dtype=F32)
    bias = jnp.einsum('xqka,hab,cnb->xhqckn', one_r, rpb.astype(F32), one_c,
                      precision=lax.Precision.HIGHEST)
    ok = row_ok[:, None, :, None, :, None] & col_ok[None, None, None, :, None, :]
    tab = jnp.where(ok, bias, NEG_INF)
    return tab.reshape(3, nh, NA_QROWS * GRID_W, NA_KROWS * GRID_W)


def _layer_weights(i, attn_norm, w_in, mla_q_norm, mla_w_q_up, mla_kv_norm, mla_w_kv_up, w_out,
                   ffn_norm, w_router, w_gate_e, w_up_e, w_down_e, ple_norm, w_ple_gate,
                   w_ple_proj):
    offs = [0]
    for n in IN_SIZES:
        offs.append(offs[-1] + n)
    parts = [w_in[i][:, offs[j]:offs[j + 1]] for j in range(len(IN_SIZES))]
    na_q, na_k, na_v, m_cq, m_ckv, m_kr, s_q, s_k, s_v = parts
    d = w_in.shape[1]
    na_scale = NA_HEAD_DIM ** -0.5
    swa_scale = SWA_HEAD_DIM ** -0.5

    zk = jnp.zeros((d, MLA_NOPE_DIM), F32)
    zp = jnp.zeros((d, MLA_HEAD_PAD - MLA_NOPE_DIM - MLA_ROPE_DIM), F32)
    w_kr = jnp.concatenate([zk, m_kr, zp, zk, _rot_cols(m_kr, MLA_ROPE_DIM), zp], axis=1)

    qu = mla_w_q_up[i].reshape(MLA_Q_RANK, MLA_HEADS, MLA_NOPE_DIM + MLA_ROPE_DIM)
    q_nope, q_rope = qu[..., :MLA_NOPE_DIM], qu[..., MLA_NOPE_DIM:]
    zq = jnp.zeros_like(q_rope)
    q_main = jnp.concatenate([q_nope, q_rope, zq], axis=-1)
    q_rot = jnp.concatenate([jnp.zeros_like(q_nope), _rot_cols(q_rope, MLA_ROPE_DIM), zq], axis=-1)
    w_q_up = jnp.concatenate([q_main.reshape(MLA_Q_RANK, -1), q_rot.reshape(MLA_Q_RANK, -1)], axis=1)

    kvu = mla_w_kv_up[i].reshape(MLA_KV_RANK, MLA_HEADS, MLA_NOPE_DIM + MLA_V_DIM)
    k_nope, v_up = kvu[..., :MLA_NOPE_DIM], kvu[..., MLA_NOPE_DIM:]
    zv = jnp.zeros_like(v_up)
    k_main = jnp.concatenate([k_nope, jnp.zeros_like(k_nope)], axis=-1)
    even = (jnp.arange(MLA_HEADS) % 2 == 0)[None, :, None]
    v_main = jnp.concatenate([jnp.where(even, v_up, zv), jnp.where(even, zv, v_up)], axis=-1)
    w_kv_up = jnp.concatenate([k_main.reshape(MLA_KV_RANK, -1), v_main.reshape(MLA_KV_RANK, -1)],
                              axis=1)

    wr = w_router[i].T
    wr_hi = wr.astype(BF16)
    wr_lo = (wr - wr_hi.astype(F32)).astype(BF16)
    wo = w_out[i]
    row = lambda g: g.reshape(1, -1).astype(F32)
    return {
        'g_attn': row(attn_norm[i]), 'g_q': row(mla_q_norm[i]), 'g_kv': row(mla_kv_norm[i]),
        'w_na': jnp.concatenate([na_q * na_scale, na_k, na_v], axis=1).astype(BF16),
        'w_mc': jnp.concatenate([m_cq, m_ckv], axis=1).astype(BF16),
        'w_kr': w_kr.astype(BF16),
        'w_swa': jnp.concatenate([s_q * swa_scale, _rot_cols(s_q, SWA_HEAD_DIM) * swa_scale,
                                  s_k, _rot_cols(s_k, SWA_HEAD_DIM), s_v], axis=1).astype(BF16),
        'w_q_up': w_q_up.astype(BF16), 'w_kv_up': w_kv_up.astype(BF16),
        'w_out_na': wo[:NA_WIDTH].astype(BF16),
        'w_out_mla': wo[NA_WIDTH:NA_WIDTH + MLA_WIDTH].astype(BF16),
        'w_out_swa': wo[NA_WIDTH + MLA_WIDTH:].astype(BF16),
        'g_ffn': row(ffn_norm[i]),
        'w_router': jnp.concatenate([wr_hi, wr_lo], axis=0),
        'w_gate': w_gate_e, 'w_up': w_up_e, 'w_down': w_down_e,
        'g_ple': row(ple_norm[i]),
        'w_ple_gate': w_ple_gate[i].astype(BF16), 'w_ple_proj': w_ple_proj[i].astype(BF16),
    }


def kernel(x, p, attn_norm, w_in, na_rpb, mla_q_norm, mla_w_q_up, mla_kv_norm, mla_w_kv_up,
           swa_sink, w_out, ffn_norm, w_router, w_gate_e, w_up_e, w_down_e, ple_norm, w_ple_gate,
           w_ple_proj, final_norm):
    b, seq, d = x.shape
    depth = w_in.shape[0]
    t = b * seq
    cap = EC_CAPACITY_FACTOR * seq // N_EXPERTS
    tabs = _rope_tables(seq)
    tri = jnp.triu(jnp.ones((LANE, LANE), F32)).astype(BF16)
    g_final = final_norm.reshape(1, -1).astype(F32)

    h = x.reshape(t, d)
    for i in range(depth):
        lw = _layer_weights(i, attn_norm, w_in, mla_q_norm, mla_w_q_up, mla_kv_norm, mla_w_kv_up,
                            w_out, ffn_norm, w_router, w_gate_e, w_up_e, w_down_e, ple_norm,
                            w_ple_gate, w_ple_proj)
        (na_q, na_k, na_v, m_q, m_k, m_v, s_q, s_k, s_v) = _inproj(h, lw, tabs, seq)
        r3 = lambda a: a.reshape(b, seq, a.shape[-1])
        o_na = _na_attention(r3(na_q), r3(na_k), r3(na_v), _na_bias_tables(na_rpb[i]))
        o_mla = _mla_attention(r3(m_q), r3(m_k), r3(m_v))
        o_swa = _swa_attention(r3(s_q), r3(s_k), r3(s_v), swa_sink[i].astype(F32))
        r2 = lambda a: a.reshape(t, a.shape[-1])
        h1, h1t, aff = _outproj(h, r2(o_na), r2(o_mla), r2(o_swa), lw, b, seq)
        idx = _select(aff, tri, cap).reshape(b, N_EXPERTS, cap)
        moe = _moe(idx, aff, h1t.reshape(b, seq * SUBLANES, LANE), lw, i, cap)
        h = _ple(h1, moe.reshape(t * SUBLANES, LANE), p[i].reshape(t, PLE_DIM), lw, g_final,
                 i == depth - 1)
    return h.reshape(b, seq, d)
```
